```python
import math
import jax, jax.numpy as jnp
from jax import lax
import numpy as np


D_MODEL = 2048
BATCH = 2
SEQ = 4096
DEPTH = 4
DEC_BATCH = 8
DEC_SEQ = 1
PAST_LEN = 16384
PAGE_SIZE = 128

D_MIX = D_MODEL
W_GROUP = D_MIX // 4
N_IN_PARTS = 10
D_IN = N_IN_PARTS * W_GROUP
CONV_A = 4
LRU_BLOCKS = 8
LRU_BLOCK_DIM = W_GROUP // LRU_BLOCKS
LRU_C = 8.0
HEAD_DIM = 64
N_HEADS_ATTN = W_GROUP // (2 * HEAD_DIM)
ROPE_THETA = 10000.0
Q_BLOCK = 128
CONV_C = 31
CONV_D = 3
N_EXPERTS = 64
TOP_K = 8
D_EXPERT = 512
D_SHARED = 512
ROUTE_SCALE = 2.5
MOE_BLOCK = 64
LN_EPS = 1e-5
ALPHA = (2.0 * DEPTH) ** 0.25
BETA = (8.0 * DEPTH) ** -0.25
NEG_INF = -1e30

kernel_name = 'hybrid_hymba_decoder_step'


def layer_norm(x, g, b):
    xf = x.astype(jnp.float32)
    mu = jnp.mean(xf, axis=-1, keepdims=True)
    var = jnp.mean(jnp.square(xf - mu), axis=-1, keepdims=True)
    return ((xf - mu) * lax.rsqrt(var + LN_EPS) * g.astype(jnp.float32) + b.astype(jnp.float32)).astype(x.dtype)


def rms_norm(x, g):
    xf = x.astype(jnp.float32)
    return (xf * lax.rsqrt(jnp.mean(xf * xf, axis=-1, keepdims=True) + LN_EPS) * g.astype(jnp.float32)).astype(x.dtype)


def causal_dwconv(x, buf, w):
    width, ch = w.shape
    xp = jnp.concatenate([buf.astype(x.dtype), x], axis=1)
    y = lax.conv_general_dilated(xp, w.astype(x.dtype)[:, None, :], window_strides=(1,), padding='VALID',
                                 dimension_numbers=('NWC', 'WIO', 'NWC'), feature_group_count=ch)
    return y, xp[:, x.shape[1]:]


def rg_lru(x, h0, w_r, b_r, w_i, b_i, lam):
    bsz, s = x.shape[0], x.shape[1]
    xb = x.reshape(bsz, s, LRU_BLOCKS, LRU_BLOCK_DIM)
    r = jax.nn.sigmoid(jnp.einsum('bshc,hcd->bshd', xb, w_r).reshape(bsz, s, W_GROUP) + b_r)
    i = jax.nn.sigmoid(jnp.einsum('bshc,hcd->bshd', xb, w_i).reshape(bsz, s, W_GROUP) + b_i)
    log_a = -LRU_C * r.astype(jnp.float32) * jax.nn.softplus(-lam.astype(jnp.float32))
    a = jnp.exp(log_a)
    u = jnp.sqrt(-jnp.expm1(2.0 * log_a)) * (i * x).astype(jnp.float32)

    def combine(left, right):
        a1, b1 = left
        a2, b2 = right
        return a1 * a2, a2 * b1 + b2

    a_cum, b_cum = lax.associative_scan(combine, (a, u), axis=1)
    h = a_cum * h0.astype(jnp.float32)[:, None, :] + b_cum
    return h.astype(x.dtype), h[:, -1].astype(x.dtype)


def rope(x, pos):
    half = HEAD_DIM // 2
    freqs = ROPE_THETA ** (-jnp.arange(half, dtype=jnp.float32) / half)
    ang = pos.astype(jnp.float32)[:, None] * freqs[None, :]
    cos = jnp.cos(ang)[None, :, None, None, :]
    sin = jnp.sin(ang)[None, :, None, None, :]
    xf = x.astype(jnp.float32)
    x1, x2 = xf[..., :half], xf[..., half:]
    return jnp.concatenate([x1 * cos - x2 * sin, x1 * sin + x2 * cos], axis=-1).astype(x.dtype)


def diff_attn_block(q, k, v, q_pos, k_pos, lam):
    s = jnp.einsum('bqhcd,bkhcd->bhcqk', q, k, preferred_element_type=jnp.float32) * (HEAD_DIM ** -0.5)
    mask = k_pos[None, :] <= q_pos[:, None]
    p = jax.nn.softmax(jnp.where(mask, s, NEG_INF), axis=-1)
    attn = p[:, :, 0] - lam * p[:, :, 1]
    return jnp.einsum('bhqk,bkhe->bqhe', attn.astype(v.dtype), v)


def hybrid_mixer(h, p, lam_init, pos, past_kv, buf_a, h_lru, buf_c, buf_d):
    bsz, s, _ = h.shape
    proj = jnp.einsum('bsd,de->bse', h, p['w_in'])
    a_x, a_gate, b_q, b_k, b_v, c_val, c_gate, d_b, d_c, d_x = jnp.split(proj, N_IN_PARTS, axis=-1)
    xa, new_buf_a = causal_dwconv(a_x, buf_a, p['conv_a_w'])
    ha, new_h = rg_lru(xa + p['conv_a_b'], h_lru, p['lru_wr'], p['lru_br'], p['lru_wi'], p['lru_bi'], p['lru_lambda'])
    y_a = jax.nn.gelu(a_gate) * ha
    q = rope(b_q.reshape(bsz, s, N_HEADS_ATTN, 2, HEAD_DIM), pos)
    k = rope(b_k.reshape(bsz, s, N_HEADS_ATTN, 2, HEAD_DIM), pos)
    v = b_v.reshape(bsz, s, N_HEADS_ATTN, 2 * HEAD_DIM)
    lam = (jnp.exp(jnp.sum(p['lam_q1'] * p['lam_k1'])) - jnp.exp(jnp.sum(p['lam_q2'] * p['lam_k2'])) + lam_init).astype(jnp.float32)
    if past_kv is None:
        n_blk = s // Q_BLOCK
        qb = q.reshape(bsz, n_blk, Q_BLOCK, N_HEADS_ATTN, 2, HEAD_DIM).swapaxes(0, 1)
        pb = pos.reshape(n_blk, Q_BLOCK)
        o = lax.map(lambda qp: diff_attn_block(qp[0], k, v, qp[1], pos, lam), (qb, pb))
        o = o.swapaxes(0, 1).reshape(bsz, s, N_HEADS_ATTN, 2 * HEAD_DIM)
    else:
        k_past, v_past = past_kv
        k_all = jnp.concatenate([k_past.astype(k.dtype).reshape(bsz, -1, N_HEADS_ATTN, 2, HEAD_DIM), k], axis=1)
        v_all = jnp.concatenate([v_past.astype(v.dtype), v], axis=1)
        k_pos = jnp.arange(k_all.shape[1], dtype=jnp.int32)
        o = diff_attn_block(q, k_all, v_all, pos, k_pos, lam)
    y_b = (rms_norm(o, p['subln_g']) * (1.0 - lam_init)).reshape(bsz, s, W_GROUP)
    uc = c_val * jax.nn.sigmoid(c_gate)
    yc, new_buf_c = causal_dwconv(uc, buf_c, p['conv_c_w'])
    y_c = jax.nn.silu(layer_norm(yc + p['conv_c_b'], p['ln_c_g'], p['ln_c_b']))
    yd, new_buf_d = causal_dwconv(d_c * d_x, buf_d, p['conv_d_w'])
    y_d = d_b * yd
    y = jnp.einsum('bse,ed->bsd', jnp.concatenate([y_a, y_b, y_c, y_d], axis=-1), p['w_out'])
    k_rows = k.reshape(bsz, s, N_HEADS_ATTN, 2 * HEAD_DIM)
    return y, (k_rows, v, new_buf_a, new_h, new_buf_c, new_buf_d)


def moe_ffn(h, p):
    bsz, s, d = h.shape
    xt = h.reshape(-1, d)
    t = xt.shape[0]
    scores = jax.nn.sigmoid(jnp.einsum('td,de->te', xt, p['router_w'], preferred_element_type=jnp.float32))
    _, idx = lax.top_k(scores + p['router_bias'].astype(jnp.float32), TOP_K)
    gate = jnp.take_along_axis(scores, idx, axis=1)
    gate = gate / jnp.sum(gate, axis=-1, keepdims=True) * ROUTE_SCALE
    n_assign = t * TOP_K
    flat_e = idx.reshape(-1)
    order = jnp.argsort(flat_e)
    sorted_e = flat_e[order]
    counts = jnp.bincount(flat_e, length=N_EXPERTS)
    padded = (counts + MOE_BLOCK - 1) // MOE_BLOCK * MOE_BLOCK
    pad_end = jnp.cumsum(padded)
    pad_start = pad_end - padded
    start = jnp.cumsum(counts) - counts
    dest = pad_start[sorted_e] + (jnp.arange(n_assign, dtype=jnp.int32) - start[sorted_e])
    n_blocks = (n_assign + MOE_BLOCK - 1) // MOE_BLOCK + N_EXPERTS
    n_rows = n_blocks * MOE_BLOCK
    tok_sorted = order // TOP_K
    row_token = jnp.full((n_rows,), t, jnp.int32).at[dest].set(tok_sorted.astype(jnp.int32))
    x_pad = jnp.concatenate([xt, jnp.zeros((1, d), xt.dtype)], axis=0)
    rows = x_pad[row_token].reshape(n_blocks, MOE_BLOCK, d)
    blk_start = jnp.arange(n_blocks, dtype=jnp.int32) * MOE_BLOCK
    blk_expert = jnp.minimum(jnp.searchsorted(pad_end, blk_start, side='right'), N_EXPERTS - 1)

    def expert_block(args):
        xb, e = args
        g = xb @ p['exp_w_gate'][e]
        u = xb @ p['exp_w_up'][e]
        return (jax.nn.silu(g) * u) @ p['exp_w_down'][e]

    out_rows = lax.map(expert_block, (rows, blk_expert)).reshape(n_rows, d)
    contrib = out_rows[dest] * gate.reshape(-1)[order][:, None].astype(out_rows.dtype)
    routed = jax.ops.segment_sum(contrib, tok_sorted, num_segments=t)
    shared = (jax.nn.silu(xt @ p['sh_w_gate']) * (xt @ p['sh_w_up'])) @ p['sh_w_down']
    return (routed + shared).reshape(bsz, s, d)


def decoder_layer(x, c, p, lam_init, pos, past_kv, buf_a, h_lru, buf_c, buf_d):
    ada = jnp.einsum('bd,de->be', jax.nn.silu(c), p['w_ada']) + p['b_ada']
    sh1, sc1, g1, sh2, sc2, g2 = jnp.split(ada[:, None, :], 6, axis=-1)
    mix, state = hybrid_mixer(x * (1.0 + sc1) + sh1, p, lam_init, pos, past_kv, buf_a, h_lru, buf_c, buf_d)
    x = layer_norm(ALPHA * x + g1 * mix, p['ln1_g'], p['ln1_b'])
    x = layer_norm(ALPHA * x + g2 * moe_ffn(x * (1.0 + sc2) + sh2, p), p['ln2_g'], p['ln2_b'])
    return x, state


def stack_state(states, i):
    return jnp.stack([st[i] for st in states], axis=0)


def setup_inputs(seed: int = 0) -> dict:
    key = jax.random.key(seed)
    ks = iter(jax.random.split(key, 64))

    def nrm(shape, scale):
        return jax.random.normal(next(ks), shape, jnp.float32) * scale

    n_pages = PAST_LEN // PAGE_SIZE
    n_pool = (DEC_BATCH * n_pages * 5) // 4
    perm = jax.random.permutation(next(ks), n_pool)
    page_table = perm[:DEC_BATCH * n_pages].reshape(DEC_BATCH, n_pages).astype(jnp.int32)
    u = jax.random.uniform(next(ks), (DEPTH, W_GROUP), jnp.float32, minval=0.9, maxval=0.999)
    a0 = u ** (1.0 / LRU_C)
    lru_lambda = jnp.log(a0) - jnp.log1p(-a0)
    return {
        'x_prompt': nrm((BATCH, SEQ, D_MODEL), 1.0),
        'x_sample': nrm((DEC_BATCH, DEC_SEQ, D_MODEL), 1.0),
        'cache_k': nrm((DEPTH, n_pool, PAGE_SIZE, N_HEADS_ATTN, 2 * HEAD_DIM), 1.0),
        'cache_v': nrm((DEPTH, n_pool, PAGE_SIZE, N_HEADS_ATTN, 2 * HEAD_DIM), 1.0),
        'state_conv_a': nrm((DEPTH, DEC_BATCH, CONV_A - 1, W_GROUP), 1.0),
        'state_lru': nrm((DEPTH, DEC_BATCH, W_GROUP), 0.5),
        'state_conv_c': nrm((DEPTH, DEC_BATCH, CONV_C - 1, W_GROUP), 0.5),
        'state_conv_d': nrm((DEPTH, DEC_BATCH, CONV_D - 1, W_GROUP), 0.5),
        'page_table': page_table,
        'c_prompt': nrm((BATCH, D_MODEL), 1.0),
        'c_sample': nrm((DEC_BATCH, D_MODEL), 1.0),
        'w_ada': nrm((DEPTH, D_MODEL, 6 * D_MODEL), 0.5 * D_MODEL ** -0.5),
        'b_ada': nrm((DEPTH, 6 * D_MODEL), 0.01),
        'w_in': nrm((DEPTH, D_MODEL, D_IN), D_MODEL ** -0.5),
        'conv_a_w': nrm((DEPTH, CONV_A, W_GROUP), CONV_A ** -0.5),
        'conv_a_b': nrm((DEPTH, W_GROUP), 0.01),
        'lru_wr': nrm((DEPTH, LRU_BLOCKS, LRU_BLOCK_DIM, LRU_BLOCK_DIM), LRU_BLOCK_DIM ** -0.5),
        'lru_br': nrm((DEPTH, W_GROUP), 0.01),
        'lru_wi': nrm((DEPTH, LRU_BLOCKS, LRU_BLOCK_DIM, LRU_BLOCK_DIM), LRU_BLOCK_DIM ** -0.5),
        'lru_bi': nrm((DEPTH, W_GROUP), 0.01),
        'lru_lambda': lru_lambda,
        'lam_q1': nrm((DEPTH, HEAD_DIM), 0.1),
        'lam_k1': nrm((DEPTH, HEAD_DIM), 0.1),
        'lam_q2': nrm((DEPTH, HEAD_DIM), 0.1),
        'lam_k2': nrm((DEPTH, HEAD_DIM), 0.1),
        'subln_g': 1.0 + nrm((DEPTH, 2 * HEAD_DIM), 0.01),
        'conv_c_w': nrm((DEPTH, CONV_C, W_GROUP), CONV_C ** -0.5),
        'conv_c_b': nrm((DEPTH, W_GROUP), 0.01),
        'ln_c_g': 1.0 + nrm((DEPTH, W_GROUP), 0.01),
        'ln_c_b': nrm((DEPTH, W_GROUP), 0.01),
        'conv_d_w': nrm((DEPTH, CONV_D, W_GROUP), CONV_D ** -0.5),
        'w_out': nrm((DEPTH, D_MIX, D_MODEL), BETA * D_MIX ** -0.5),
        'ln1_g': 1.0 + nrm((DEPTH, D_MODEL), 0.01),
        'ln1_b': nrm((DEPTH, D_MODEL), 0.01),
        'ln2_g': 1.0 + nrm((DEPTH, D_MODEL), 0.01),
        'ln2_b': nrm((DEPTH, D_MODEL), 0.01),
        'router_w': nrm((DEPTH, D_MODEL, N_EXPERTS), D_MODEL ** -0.5),
        'router_bias': nrm((DEPTH, N_EXPERTS), 0.01),
        'exp_w_gate': nrm((DEPTH, N_EXPERTS, D_MODEL, D_EXPERT), D_MODEL ** -0.5),
        'exp_w_up': nrm((DEPTH, N_EXPERTS, D_MODEL, D_EXPERT), D_MODEL ** -0.5),
        'exp_w_down': nrm((DEPTH, N_EXPERTS, D_EXPERT, D_MODEL), BETA * D_EXPERT ** -0.5),
        'sh_w_gate': nrm((DEPTH, D_MODEL, D_SHARED), D_MODEL ** -0.5),
        'sh_w_up': nrm((DEPTH, D_MODEL, D_SHARED), D_MODEL ** -0.5),
        'sh_w_down': nrm((DEPTH, D_SHARED, D_MODEL), BETA * D_SHARED ** -0.5),
    }


def reference(x_prompt, x_sample, cache_k, cache_v, state_conv_a, state_lru, state_conv_c, state_conv_d,
              page_table, c_prompt, c_sample, w_ada, b_ada, w_in, conv_a_w, conv_a_b, lru_wr, lru_br,
              lru_wi, lru_bi, lru_lambda, lam_q1, lam_k1, lam_q2, lam_k2, subln_g, conv_c_w, conv_c_b,
              ln_c_g, ln_c_b, conv_d_w, w_out, ln1_g, ln1_b, ln2_g, ln2_b, router_w, router_bias,
              exp_w_gate, exp_w_up, exp_w_down, sh_w_gate, sh_w_up, sh_w_down):
    pos_p = jnp.arange(SEQ, dtype=jnp.int32)
    pos_s = PAST_LEN + jnp.arange(DEC_SEQ, dtype=jnp.int32)
    db = x_sample.shape[0]
    dt = x_prompt.dtype
    xp, xs = x_prompt, x_sample
    st_p, st_s = [], []
    for l in range(DEPTH):
        p = dict(w_ada=w_ada[l], b_ada=b_ada[l], w_in=w_in[l], conv_a_w=conv_a_w[l], conv_a_b=conv_a_b[l],
                 lru_wr=lru_wr[l], lru_br=lru_br[l], lru_wi=lru_wi[l], lru_bi=lru_bi[l], lru_lambda=lru_lambda[l],
                 lam_q1=lam_q1[l], lam_k1=lam_k1[l], lam_q2=lam_q2[l], lam_k2=lam_k2[l], subln_g=subln_g[l],
                 conv_c_w=conv_c_w[l], conv_c_b=conv_c_b[l], ln_c_g=ln_c_g[l], ln_c_b=ln_c_b[l],
                 conv_d_w=conv_d_w[l], w_out=w_out[l], ln1_g=ln1_g[l], ln1_b=ln1_b[l], ln2_g=ln2_g[l],
                 ln2_b=ln2_b[l], router_w=router_w[l], router_bias=router_bias[l], exp_w_gate=exp_w_gate[l],
                 exp_w_up=exp_w_up[l], exp_w_down=exp_w_down[l], sh_w_gate=sh_w_gate[l], sh_w_up=sh_w_up[l],
                 sh_w_down=sh_w_down[l])
        lam_init = 0.8 - 0.6 * math.exp(-0.3 * l)
        xp, s_p = decoder_layer(xp, c_prompt, p, lam_init, pos_p, None,
                                jnp.zeros((BATCH, CONV_A - 1, W_GROUP), dt), jnp.zeros((BATCH, W_GROUP), dt),
                                jnp.zeros((BATCH, CONV_C - 1, W_GROUP), dt), jnp.zeros((BATCH, CONV_D - 1, W_GROUP), dt))
        st_p.append(s_p)
        k_past = cache_k[l, page_table].reshape(db, -1, N_HEADS_ATTN, 2 * HEAD_DIM)
        v_past = cache_v[l, page_table].reshape(db, -1, N_HEADS_ATTN, 2 * HEAD_DIM)
        xs, s_s = decoder_layer(xs, c_sample, p, lam_init, pos_s, (k_past, v_past),
                                state_conv_a[l], state_lru[l], state_conv_c[l], state_conv_d[l])
        st_s.append(s_s)
    new_k_prompt = stack_state(st_p, 0).reshape(DEPTH, BATCH, SEQ // PAGE_SIZE, PAGE_SIZE, N_HEADS_ATTN, 2 * HEAD_DIM)
    new_v_prompt = stack_state(st_p, 1).reshape(DEPTH, BATCH, SEQ // PAGE_SIZE, PAGE_SIZE, N_HEADS_ATTN, 2 * HEAD_DIM)
    new_conv_a_prompt = stack_state(st_p, 2)
    new_lru_prompt = stack_state(st_p, 3)
    new_conv_c_prompt = stack_state(st_p, 4)
    new_conv_d_prompt = stack_state(st_p, 5)
    new_k_sample = stack_state(st_s, 0)
    new_v_sample = stack_state(st_s, 1)
    new_conv_a_sample = stack_state(st_s, 2)
    new_lru_sample = stack_state(st_s, 3)
    new_conv_c_sample = stack_state(st_s, 4)
    new_conv_d_sample = stack_state(st_s, 5)
    return (xp, xs, new_k_prompt, new_v_prompt, new_conv_a_prompt, new_lru_prompt, new_conv_c_prompt,
            new_conv_d_prompt, new_k_sample, new_v_sample, new_conv_a_sample, new_lru_sample,
            new_conv_c_sample, new_conv_d_sample)
```

```python
import functools
import math

import jax
import jax.numpy as jnp
from jax import lax
from jax.experimental import pallas as pl
from jax.experimental.pallas import tpu as pltpu

F32 = jnp.float32
BF16 = jnp.bfloat16

D_MODEL = 2048
BATCH = 2
SEQ = 4096
DEPTH = 4
DEC_BATCH = 8
PAST_LEN = 16384
PAGE_SIZE = 128
W_GROUP = 512
N_IN_PARTS = 10
D_IN = N_IN_PARTS * W_GROUP
CONV_A = 4
LRU_BLOCKS = 8
LRU_BLOCK_DIM = W_GROUP // LRU_BLOCKS
LRU_C = 8.0
HEAD_DIM = 64
N_HEADS = 4
ROPE_THETA = 10000.0
CONV_C = 31
CONV_D = 3
N_EXPERTS = 64
TOP_K = 8
D_EXPERT = 512
ROUTE_SCALE = 2.5
LN_EPS = 1e-5
ALPHA = (2.0 * DEPTH) ** 0.25
NEG_INF = -1e30

LANES = 128
VMEM_LIMIT = 56 * 1024 * 1024

T_PROMPT = BATCH * SEQ
T_ALL = T_PROMPT + DEC_BATCH
MOE_BM = 256
MOE_NB = -(-T_ALL * TOP_K // MOE_BM) + N_EXPERTS
MOE_ROWS = MOE_NB * MOE_BM
N_PAGES = PAST_LEN // PAGE_SIZE
DEC_G = 8


def _cparams(sem, **kw):
    return pltpu.CompilerParams(dimension_semantics=sem, vmem_limit_bytes=VMEM_LIMIT, **kw)


def _dot(a, b):
    return jnp.dot(a, b, preferred_element_type=F32)


def _dot_nt(a, b):
    return lax.dot_general(a, b, (((1,), (1,)), ((), ())), preferred_element_type=F32)


def _split_bf16(x):
    hi = x.astype(BF16)
    lo = (x - hi.astype(F32)).astype(BF16)
    return hi, lo


def _dot3(a, b):
    ah, al = _split_bf16(a)
    bh, bl = _split_bf16(b)
    return _dot(ah, bh) + _dot(ah, bl) + _dot(al, bh)


def _sigmoid(x):
    return 1.0 / (1.0 + jnp.exp(-x))


def _silu(x):
    return x * _sigmoid(x)


def _layer_norm(z, g, b):
    mu = jnp.mean(z, axis=-1, keepdims=True)
    zc = z - mu
    var = jnp.mean(zc * zc, axis=-1, keepdims=True)
    return zc * lax.rsqrt(var + LN_EPS) * g + b


def _ada_kernel(c_ref, w_ref, b_ref, o_ref):
    c = c_ref[...]
    o_ref[...] = _dot3(_silu(c), w_ref[...]) + b_ref[...]


def _ada_call(c_all, w_ada, b_ada):
    r = c_all.shape[0]
    tn = 1024
    return pl.pallas_call(
        _ada_kernel,
        out_shape=jax.ShapeDtypeStruct((DEPTH, r, 6 * D_MODEL), F32),
        grid=(DEPTH, 6 * D_MODEL // tn),
        in_specs=[
            pl.BlockSpec((r, D_MODEL), lambda l, n: (0, 0)),
            pl.BlockSpec((None, D_MODEL, tn), lambda l, n: (l, 0, n)),
            pl.BlockSpec((None, 1, tn), lambda l, n: (l, 0, n)),
        ],
        out_specs=pl.BlockSpec((None, r, tn), lambda l, n: (l, 0, n)),
        compiler_params=_cparams(("arbitrary", "arbitrary")),
        name="ada",
    )(c_all, w_ada, b_ada.reshape(DEPTH, 1, 6 * D_MODEL))


def _inproj_kernel(x_ref, sc_ref, sh_ref, w_ref, o_ref, h_scr):
    @pl.when(pl.program_id(1) == 0)
    def _():
        h_scr[...] = (x_ref[...] * (1.0 + sc_ref[...]) + sh_ref[...]).astype(BF16)

    o_ref[...] = _dot(h_scr[...], w_ref[...])


def _mod_spec(rows_per_b, tm, col):
    def imap(m, *_):
        return ((m * tm) // rows_per_b, 0, col)
    return imap


def _inproj_call(x, ada, w_in_bf, layer, rows_per_b, tm, tn):
    m = x.shape[0]
    r = ada.shape[1]
    return pl.pallas_call(
        _inproj_kernel,
        out_shape=jax.ShapeDtypeStruct((m, D_IN), F32),
        grid=(m // tm, D_IN // tn),
        in_specs=[
            pl.BlockSpec((tm, D_MODEL), lambda i, n: (i, 0)),
            pl.BlockSpec((None, r, D_MODEL), _mod_spec(rows_per_b, tm, 1)),
            pl.BlockSpec((None, r, D_MODEL), _mod_spec(rows_per_b, tm, 0)),
            pl.BlockSpec((None, D_MODEL, tn), lambda i, n: (layer, 0, n)),
        ],
        out_specs=pl.BlockSpec((tm, tn), lambda i, n: (i, n)),
        scratch_shapes=[pltpu.VMEM((tm, D_MODEL), BF16)],
        compiler_params=_cparams(("arbitrary", "arbitrary")),
        name="inproj",
    )(x, ada, ada, w_in_bf)


def _softplus(z):
    return jnp.maximum(z, 0.0) + jnp.log1p(jnp.exp(-jnp.abs(z)))


def _gelu_tanh(x):
    return 0.5 * x * (1.0 + jnp.tanh(math.sqrt(2.0 / math.pi) * (x + 0.044715 * (x * x * x))))


def _lru_gates(xa, wr, br, wi, bi, lam):
    xb = xa.astype(BF16)
    r = _sigmoid(_dot(xb, wr) + br)
    i = _sigmoid(_dot(xb, wi) + bi)
    log_a = -LRU_C * r * _softplus(-lam)
    a = jnp.exp(log_a)
    u = jnp.sqrt(jnp.tanh(-log_a) * (1.0 + a * a)) * (i * xa)
    return a, u


def _rope(x, cos_t, sin_t):
    lane = lax.broadcasted_iota(jnp.int32, x.shape, 1)
    first_half = (lane & (HEAD_DIM // 2)) == 0
    w = x.shape[1]
    partner = jnp.where(first_half, pltpu.roll(x, w - HEAD_DIM // 2, axis=1), pltpu.roll(x, HEAD_DIM // 2, axis=1))
    return x * cos_t + partner * sin_t


def _tile4(t):
    return jnp.concatenate([t, t, t, t], axis=1)


def _conv_from_buf(buf_ref, w_ref, hist, n_taps, rows, rb):
    outs = []
    for r0 in range(0, rows, rb):
        acc = None
        for j in range(n_taps):
            term = buf_ref[pl.ds(hist - (n_taps - 1) + j + r0, rb), :] * w_ref[pl.ds(j, 1), :]
            acc = term if acc is None else acc + term
        outs.append(acc)
    return jnp.concatenate(outs, axis=0) if len(outs) > 1 else outs[0]


def _linear_scan(a, b):
    n = a.shape[0]
    rows = lax.broadcasted_iota(jnp.int32, a.shape, 0)
    d = 1
    while d < n:
        keep = rows >= d
        a_sh = jnp.where(keep, pltpu.roll(a, d, axis=0), 1.0)
        b_sh = jnp.where(keep, pltpu.roll(b, d, axis=0), 0.0)
        b = a * b_sh + b
        a = a * a_sh
        d *= 2
    return a, b


MIX_TC = 256
HIST_A = 8
HIST_C = 32
HIST_D = 8
CONV_RB = 32


def _mixer_prompt_kernel(ax_ref, ag_ref, bq_ref, bk_ref, cv_ref, cg_ref, db_ref, dc_ref, dx_ref,
                         cos_ref, sin_ref, caw_ref, cab_ref, wr_ref, br_ref, wi_ref, bi_ref, lam_ref,
                         ccw_ref, ccb_ref, lng_ref, lnb_ref, cdw_ref,
                         ya_ref, yc_ref, yd_ref, q_ref, k_ref, sa_ref, sh_ref, sc_ref, sd_ref,
                         xa_buf, uc_buf, dd_buf, h_buf):
    tc = MIX_TC

    @pl.when(pl.program_id(1) == 0)
    def _():
        xa_buf[pl.ds(0, HIST_A), :] = jnp.zeros((HIST_A, W_GROUP), F32)
        uc_buf[pl.ds(0, HIST_C), :] = jnp.zeros((HIST_C, W_GROUP), F32)
        dd_buf[pl.ds(0, HIST_D), :] = jnp.zeros((HIST_D, W_GROUP), F32)
        h_buf[...] = jnp.zeros((1, W_GROUP), F32)

    xa_buf[pl.ds(HIST_A, tc), :] = ax_ref[...]
    xa = _conv_from_buf(xa_buf, caw_ref, HIST_A, CONV_A, tc, CONV_RB) + cab_ref[...]
    a, u = _lru_gates(xa, wr_ref[...], br_ref[...], wi_ref[...], bi_ref[...], lam_ref[...])
    a_cum, b_cum = _linear_scan(a, u)
    h = a_cum * h_buf[...] + b_cum
    ya_ref[...] = (_gelu_tanh(ag_ref[...]) * h).astype(BF16)
    h_last = h[tc - 1:tc, :]
    h_buf[...] = h_last
    sh_ref[...] = h_last
    sa_ref[...] = xa_buf[pl.ds(HIST_A + tc - (CONV_A - 1), CONV_A - 1), :]
    xa_buf[pl.ds(0, HIST_A), :] = xa_buf[pl.ds(tc, HIST_A), :]

    uc_buf[pl.ds(HIST_C, tc), :] = cv_ref[...] * _sigmoid(cg_ref[...])
    yc = _conv_from_buf(uc_buf, ccw_ref, HIST_C, CONV_C, tc, CONV_RB) + ccb_ref[...]
    yc_ref[...] = _silu(_layer_norm(yc, lng_ref[...], lnb_ref[...])).astype(BF16)
    sc_ref[...] = uc_buf[pl.ds(HIST_C + tc - (CONV_C - 1), CONV_C - 1), :]
    uc_buf[pl.ds(0, HIST_C), :] = uc_buf[pl.ds(tc, HIST_C), :]

    dd_buf[pl.ds(HIST_D, tc), :] = dc_ref[...] * dx_ref[...]
    yd = _conv_from_buf(dd_buf, cdw_ref, HIST_D, CONV_D, tc, CONV_RB)
    yd_ref[...] = (db_ref[...] * yd).astype(BF16)
    sd_ref[...] = dd_buf[pl.ds(HIST_D + tc - (CONV_D - 1), CONV_D - 1), :]
    dd_buf[pl.ds(0, HIST_D), :] = dd_buf[pl.ds(tc, HIST_D), :]

    cos_t = _tile4(cos_ref[...])
    sin_t = _tile4(sin_ref[...])
    q_ref[...] = (_rope(bq_ref[...], cos_t, sin_t) * (HEAD_DIM ** -0.5)).astype(BF16)
    k_ref[...] = _rope(bk_ref[...], cos_t, sin_t)


def _mixer_prompt_call(proj, cos_t, sin_t, prm):
    tc = MIX_TC

    def col(j):
        return pl.BlockSpec((None, tc, W_GROUP), lambda b, c: (b, c, j))

    def full(arr):
        nd = arr.ndim
        return pl.BlockSpec(arr.shape, lambda b, c: (0,) * nd)

    small = [prm["conv_a_w"], prm["conv_a_b"], prm["wr"], prm["br"], prm["wi"], prm["bi"], prm["lam"],
             prm["conv_c_w"], prm["conv_c_b"], prm["ln_c_g"], prm["ln_c_b"], prm["conv_d_w"]]
    tok = pl.BlockSpec((None, tc, W_GROUP), lambda b, c: (b, c, 0))

    def state(n):
        return pl.BlockSpec((None, n, W_GROUP), lambda b, c: (b, 0, 0))

    out_shape = (
        jax.ShapeDtypeStruct((BATCH, SEQ, W_GROUP), BF16),
        jax.ShapeDtypeStruct((BATCH, SEQ, W_GROUP), BF16),
        jax.ShapeDtypeStruct((BATCH, SEQ, W_GROUP), BF16),
        jax.ShapeDtypeStruct((BATCH, SEQ, W_GROUP), BF16),
        jax.ShapeDtypeStruct((BATCH, SEQ, W_GROUP), F32),
        jax.ShapeDtypeStruct((BATCH, CONV_A - 1, W_GROUP), F32),
        jax.ShapeDtypeStruct((BATCH, 1, W_GROUP), F32),
        jax.ShapeDtypeStruct((BATCH, CONV_C - 1, W_GROUP), F32),
        jax.ShapeDtypeStruct((BATCH, CONV_D - 1, W_GROUP), F32),
    )
    return pl.pallas_call(
        _mixer_prompt_kernel,
        out_shape=out_shape,
        grid=(BATCH, SEQ // tc),
        in_specs=[col(0), col(1), col(2), col(3), col(5), col(6), col(7), col(8), col(9),
                  pl.BlockSpec((tc, LANES), lambda b, c: (c, 0)),
                  pl.BlockSpec((tc, LANES), lambda b, c: (c, 0))] + [full(a) for a in small],
        out_specs=(tok, tok, tok, tok, tok, state(CONV_A - 1), state(1), state(CONV_C - 1), state(CONV_D - 1)),
        scratch_shapes=[pltpu.VMEM((HIST_A + tc, W_GROUP), F32), pltpu.VMEM((HIST_C + tc, W_GROUP), F32),
                        pltpu.VMEM((HIST_D + tc, W_GROUP), F32), pltpu.VMEM((1, W_GROUP), F32)],
        compiler_params=_cparams(("arbitrary", "arbitrary")),
        name="mixer_prompt",
    )(*([proj] * 9), cos_t, sin_t, *small)


def _mixer_sample_kernel(ax_ref, ag_ref, bq_ref, bk_ref, cv_ref, cg_ref, db_ref, dc_ref, dx_ref,
                         cos_ref, sin_ref, caw_ref, cab_ref, wr_ref, br_ref, wi_ref, bi_ref, lam_ref,
                         ccw_ref, ccb_ref, lng_ref, lnb_ref, cdw_ref,
                         ba_ref, h0_ref, bc_ref, bd_ref,
                         ya_ref, yc_ref, yd_ref, q_ref, k_ref, sa_ref, sh_ref, sc_ref, sd_ref):
    ax = ax_ref[...]
    xa = cab_ref[...] + caw_ref[pl.ds(CONV_A - 1, 1), :] * ax
    for j in range(CONV_A - 1):
        xa = xa + caw_ref[pl.ds(j, 1), :] * ba_ref[j]
    a, u = _lru_gates(xa, wr_ref[...], br_ref[...], wi_ref[...], bi_ref[...], lam_ref[...])
    h = a * h0_ref[...] + u
    ya_ref[...] = (_gelu_tanh(ag_ref[...]) * h).astype(BF16)
    sh_ref[...] = h
    for j in range(CONV_A - 2):
        sa_ref[j] = ba_ref[j + 1]
    sa_ref[CONV_A - 2] = ax

    uc = cv_ref[...] * _sigmoid(cg_ref[...])
    yc = ccb_ref[...] + ccw_ref[pl.ds(CONV_C - 1, 1), :] * uc
    for j in range(CONV_C - 1):
        yc = yc + ccw_ref[pl.ds(j, 1), :] * bc_ref[j]
    yc_ref[...] = _silu(_layer_norm(yc, lng_ref[...], lnb_ref[...])).astype(BF16)
    for j in range(CONV_C - 2):
        sc_ref[j] = bc_ref[j + 1]
    sc_ref[CONV_C - 2] = uc

    dd = dc_ref[...] * dx_ref[...]
    yd = cdw_ref[pl.ds(CONV_D - 1, 1), :] * dd
    for j in range(CONV_D - 1):
        yd = yd + cdw_ref[pl.ds(j, 1), :] * bd_ref[j]
    yd_ref[...] = (db_ref[...] * yd).astype(BF16)
    for j in range(CONV_D - 2):
        sd_ref[j] = bd_ref[j + 1]
    sd_ref[CONV_D - 2] = dd

    cos_t = _tile4(cos_ref[...])
    sin_t = _tile4(sin_ref[...])
    q_ref[...] = _rope(bq_ref[...], cos_t, sin_t)
    k_ref[...] = _rope(bk_ref[...], cos_t, sin_t)


def _mixer_sample_call(proj, cos_t, sin_t, prm, buf_a, h0, buf_c, buf_d):
    nb = DEC_BATCH

    def col(j):
        return pl.BlockSpec((nb, W_GROUP), lambda i: (0, j))

    def full(arr):
        nd = arr.ndim
        return pl.BlockSpec(arr.shape, lambda i: (0,) * nd)

    small = [prm["conv_a_w"], prm["conv_a_b"], prm["wr"], prm["br"], prm["wi"], prm["bi"], prm["lam"],
             prm["conv_c_w"], prm["conv_c_b"], prm["ln_c_g"], prm["ln_c_b"], prm["conv_d_w"]]
    states = [buf_a, h0, buf_c, buf_d]
    tok_bf = jax.ShapeDtypeStruct((nb, W_GROUP), BF16)
    tok_f = jax.ShapeDtypeStruct((nb, W_GROUP), F32)
    out_shape = (tok_bf, tok_bf, tok_bf, tok_f, tok_f,
                 jax.ShapeDtypeStruct(buf_a.shape, F32), tok_f,
                 jax.ShapeDtypeStruct(buf_c.shape, F32), jax.ShapeDtypeStruct(buf_d.shape, F32))
    return pl.pallas_call(
        _mixer_sample_kernel,
        out_shape=out_shape,
        grid=(1,),
        in_specs=[col(0), col(1), col(2), col(3), col(5), col(6), col(7), col(8), col(9),
                  full(cos_t), full(sin_t)] + [full(a) for a in small] + [full(a) for a in states],
        out_specs=tuple(pl.BlockSpec(s.shape, lambda i, nd=len(s.shape): (0,) * nd) for s in out_shape),
        compiler_params=_cparams(("arbitrary",)),
        name="mixer_sample",
    )(*([proj] * 9), cos_t, sin_t, *small, *states)


def _diff_lambda(lq1_ref, lk1_ref, lq2_ref, lk2_ref, lam_init):
    s1 = jnp.sum(lq1_ref[...] * lk1_ref[...], axis=-1, keepdims=True)
    s2 = jnp.sum(lq2_ref[...] * lk2_ref[...], axis=-1, keepdims=True)
    return jnp.exp(s1) - jnp.exp(s2) + lam_init


def _diff_finish(o1, o2, lam, g, lam_init):
    od = o1 - lam * o2
    ms = jnp.mean(od * od, axis=-1, keepdims=True)
    return od * lax.rsqrt(ms + LN_EPS) * g * (1.0 - lam_init)


FLASH_TQ = 256


def _flash_kernel(q_ref, k_ref, v_ref, lq1_ref, lk1_ref, lq2_ref, lk2_ref, g_ref, li_ref, o_ref,
                  m_scr, l_scr, acc_scr):
    tq = FLASH_TQ
    i = pl.program_id(2)
    q = q_ref[...]
    lane = lax.broadcasted_iota(jnp.int32, q.shape, 1)
    zero = jnp.zeros_like(q)
    qq = jnp.concatenate([jnp.where(lane < HEAD_DIM, q, zero), jnp.where(lane >= HEAD_DIM, q, zero)], axis=0)
    m_scr[...] = jnp.full(m_scr.shape, NEG_INF, F32)
    l_scr[...] = jnp.zeros(l_scr.shape, F32)
    acc_scr[...] = jnp.zeros(acc_scr.shape, F32)

    def step(j, masked):
        start = pl.multiple_of(j * tq, tq)
        k = k_ref[pl.ds(start, tq), :].astype(BF16)
        v = v_ref[pl.ds(start, tq), :].astype(BF16)
        s = _dot_nt(qq, k)
        if masked:
            r = lax.broadcasted_iota(jnp.int32, s.shape, 0)
            c = lax.broadcasted_iota(jnp.int32, s.shape, 1)
            s = jnp.where(c <= jnp.where(r >= tq, r - tq, r), s, NEG_INF)
        m_prev = m_scr[...]
        m_new = jnp.maximum(m_prev, jnp.max(s, axis=1, keepdims=True))
        alpha = jnp.exp(m_prev - m_new)
        p = jnp.exp(s - m_new)
        l_scr[...] = alpha * l_scr[...] + jnp.sum(p, axis=1, keepdims=True)
        acc_scr[...] = alpha * acc_scr[...] + _dot(p.astype(BF16), v)
        m_scr[...] = m_new

    def body(j, carry):
        step(j, False)
        return carry

    lax.fori_loop(0, i, body, 0)
    step(i, True)

    o = acc_scr[...] / l_scr[...]
    lam_init = li_ref[...]
    lam = _diff_lambda(lq1_ref, lk1_ref, lq2_ref, lk2_ref, lam_init)
    o_ref[...] = _diff_finish(o[:tq], o[tq:], lam, g_ref[...], lam_init).astype(BF16)


def _flash_call(q, k, proj, prm):
    tq = FLASH_TQ
    v_col0 = 4 * W_GROUP // LANES

    def full(arr):
        nd = arr.ndim
        return pl.BlockSpec(arr.shape, lambda b, h, i: (0,) * nd)

    small = [prm["lam_q1"], prm["lam_k1"], prm["lam_q2"], prm["lam_k2"], prm["subln_g"], prm["lam_init"]]
    return pl.pallas_call(
        _flash_kernel,
        out_shape=jax.ShapeDtypeStruct((BATCH, SEQ, W_GROUP), BF16),
        grid=(BATCH, N_HEADS, SEQ // tq),
        in_specs=[pl.BlockSpec((None, tq, LANES), lambda b, h, i: (b, i, h)),
                  pl.BlockSpec((None, SEQ, LANES), lambda b, h, i: (b, 0, h)),
                  pl.BlockSpec((None, SEQ, LANES), lambda b, h, i: (b, 0, v_col0 + h))] + [full(a) for a in small],
        out_specs=pl.BlockSpec((None, tq, LANES), lambda b, h, i: (b, i, h)),
        scratch_shapes=[pltpu.VMEM((2 * tq, 1), F32), pltpu.VMEM((2 * tq, 1), F32), pltpu.VMEM((2 * tq, LANES), F32)],
        compiler_params=_cparams(("arbitrary", "arbitrary", "arbitrary")),
        name="flash",
    )(q, k, proj, *small)


def _softmax_update(s, v, m_scr, l_scr, acc_scr):
    m_prev = m_scr[...]
    m_new = jnp.maximum(m_prev, jnp.max(s, axis=1, keepdims=True))
    alpha = jnp.exp(m_prev - m_new)
    p = jnp.exp(s - m_new)
    l_scr[...] = alpha * l_scr[...] + jnp.sum(p, axis=1, keepdims=True)
    acc_scr[...] = alpha * acc_scr[...] + _dot(p.astype(BF16), v)
    m_scr[...] = m_new


def _decode_kernel(pt_ref, q_ref, *refs):
    k_refs = refs[:DEC_G]
    v_refs = refs[DEC_G:2 * DEC_G]
    (kn_ref, vn_ref, lq1_ref, lk1_ref, lq2_ref, lk2_ref, g_ref, li_ref, o_ref, m_scr, l_scr, acc_scr) = refs[2 * DEC_G:]
    s_idx = pl.program_id(1)

    @pl.when(s_idx == 0)
    def _():
        m_scr[...] = jnp.full(m_scr.shape, NEG_INF, F32)
        l_scr[...] = jnp.zeros(l_scr.shape, F32)
        acc_scr[...] = jnp.zeros(acc_scr.shape, F32)

    q = q_ref[...]
    n_kv = PAGE_SIZE * N_HEADS
    row = lax.broadcasted_iota(jnp.int32, (2 * N_HEADS, n_kv), 0)
    col = lax.broadcasted_iota(jnp.int32, (2 * N_HEADS, n_kv), 1)
    same_head = (col & (N_HEADS - 1)) == (row & (N_HEADS - 1))
    for g in range(DEC_G):
        k = k_refs[g][...].astype(BF16)
        v = v_refs[g][...].astype(BF16)
        s = jnp.where(same_head, _dot_nt(q, k), NEG_INF)
        _softmax_update(s, v, m_scr, l_scr, acc_scr)

    @pl.when(s_idx == pl.num_programs(1) - 1)
    def _():
        kn = kn_ref[...].astype(BF16)
        vn = vn_ref[...].astype(BF16)
        r8 = lax.broadcasted_iota(jnp.int32, (2 * N_HEADS, 2 * N_HEADS), 0)
        c8 = lax.broadcasted_iota(jnp.int32, (2 * N_HEADS, 2 * N_HEADS), 1)
        s = jnp.where(c8 == (r8 & (N_HEADS - 1)), _dot_nt(q, kn), NEG_INF)
        _softmax_update(s, vn, m_scr, l_scr, acc_scr)
        o = acc_scr[...] / l_scr[...]
        lam_init = li_ref[...]
        lam = _diff_lambda(lq1_ref, lk1_ref, lq2_ref, lk2_ref, lam_init)
        o_ref[...] = _diff_finish(o[:N_HEADS], o[N_HEADS:], lam, g_ref[...], lam_init)


def _decode_call(q8, k_new8, v_new8, cache_k4, cache_v4, page_table, prm, layer):
    n_kv = PAGE_SIZE * N_HEADS
    steps = N_PAGES // DEC_G

    def page_spec(g):
        return pl.BlockSpec((None, None, n_kv, LANES), lambda b, s, pt: (layer, pt[b, s * DEC_G + g], 0, 0))

    def full(arr):
        nd = arr.ndim
        return pl.BlockSpec(arr.shape, lambda b, s, pt: (0,) * nd)

    row8 = pl.BlockSpec((None, 2 * N_HEADS, LANES), lambda b, s, pt: (b, 0, 0))
    small = [prm["lam_q1"], prm["lam_k1"], prm["lam_q2"], prm["lam_k2"], prm["subln_g"], prm["lam_init"]]
    grid_spec = pltpu.PrefetchScalarGridSpec(
        num_scalar_prefetch=1,
        grid=(DEC_BATCH, steps),
        in_specs=[row8] + [page_spec(g) for g in range(DEC_G)] + [page_spec(g) for g in range(DEC_G)]
                 + [row8, row8] + [full(a) for a in small],
        out_specs=pl.BlockSpec((None, N_HEADS, LANES), lambda b, s, pt: (b, 0, 0)),
        scratch_shapes=[pltpu.VMEM((2 * N_HEADS, 1), F32), pltpu.VMEM((2 * N_HEADS, 1), F32),
                        pltpu.VMEM((2 * N_HEADS, LANES), F32)],
    )
    return pl.pallas_call(
        _decode_kernel,
        out_shape=jax.ShapeDtypeStruct((DEC_BATCH, N_HEADS, LANES), F32),
        grid_spec=grid_spec,
        compiler_params=_cparams(("arbitrary", "arbitrary")),
        name="decode_attn",
    )(page_table, q8, *([cache_k4] * DEC_G), *([cache_v4] * DEC_G), k_new8, v_new8, *small)


def _outproj_kernel(ya_ref, yb_ref, yc_ref, yd_ref, w_ref, x_ref, g1_ref, sc2_ref, sh2_ref, lg_ref, lb_ref,
                    rw_ref, rb_ref, x1_ref, h2_ref, idx_ref, gate_ref):
    mix = (_dot(ya_ref[...], w_ref[pl.ds(0, W_GROUP), :]) + _dot(yb_ref[...], w_ref[pl.ds(W_GROUP, W_GROUP), :])
           + _dot(yc_ref[...], w_ref[pl.ds(2 * W_GROUP, W_GROUP), :]) + _dot(yd_ref[...], w_ref[pl.ds(3 * W_GROUP, W_GROUP), :]))
    x1 = _layer_norm(ALPHA * x_ref[...] + g1_ref[...] * mix, lg_ref[...], lb_ref[...])
    x1_ref[...] = x1
    h2 = x1 * (1.0 + sc2_ref[...]) + sh2_ref[...]
    h2_ref[...] = h2

    scores = _sigmoid(_dot3(h2, rw_ref[...]))
    sel = scores + rb_ref[...]
    tm = scores.shape[0]
    lane_e = lax.broadcasted_iota(jnp.int32, (tm, N_EXPERTS), 1).astype(F32)
    lane_o = lax.broadcasted_iota(jnp.int32, (tm, LANES), 1)
    idx_acc = jnp.zeros((tm, LANES), F32)
    gate_acc = jnp.zeros((tm, LANES), F32)
    gsum = jnp.zeros((tm, 1), F32)
    for k in range(TOP_K):
        best = jnp.max(sel, axis=1, keepdims=True)
        ik = jnp.min(jnp.where(sel == best, lane_e, float(N_EXPERTS)), axis=1, keepdims=True)
        hit = lane_e == ik
        gk = jnp.sum(jnp.where(hit, scores, 0.0), axis=1, keepdims=True)
        sel = jnp.where(hit, -jnp.inf, sel)
        idx_acc = jnp.where(lane_o == k, ik, idx_acc)
        gate_acc = jnp.where(lane_o == k, gk, gate_acc)
        gsum = gsum + gk
    idx_ref[...] = idx_acc.astype(jnp.int32)
    gate_ref[...] = gate_acc / gsum * ROUTE_SCALE


def _outproj_call(ya, yb, yc, yd, w_out_bf, x, ada, prm, layer, rows_per_b, tm):
    m = x.shape[0]
    r = ada.shape[1]

    def full(arr):
        nd = arr.ndim
        return pl.BlockSpec(arr.shape, lambda i: (0,) * nd)

    grp = pl.BlockSpec((tm, W_GROUP), lambda i: (i, 0))
    row = pl.BlockSpec((tm, D_MODEL), lambda i: (i, 0))
    lane_out = pl.BlockSpec((tm, LANES), lambda i: (i, 0))

    def mod(colidx):
        return pl.BlockSpec((None, r, D_MODEL), _mod_spec(rows_per_b, tm, colidx))

    small = [prm["ln1_g"], prm["ln1_b"], prm["router_w"], prm["router_bias"]]
    return pl.pallas_call(
        _outproj_kernel,
        out_shape=(jax.ShapeDtypeStruct((m, D_MODEL), F32), jax.ShapeDtypeStruct((m, D_MODEL), F32),
                   jax.ShapeDtypeStruct((m, LANES), jnp.int32), jax.ShapeDtypeStruct((m, LANES), F32)),
        grid=(m // tm,),
        in_specs=[grp, grp, grp, grp,
                  pl.BlockSpec((None, D_MODEL, D_MODEL), lambda i: (layer, 0, 0)),
                  row, mod(2), mod(4), mod(3)] + [full(a) for a in small],
        out_specs=(row, row, lane_out, lane_out),
        compiler_params=_cparams(("arbitrary",)),
        name="outproj_ln1_router",
    )(ya, yb, yc, yd, w_out_bf, x, ada, ada, ada, *small)


def _moe_plan(idx_all):
    onehot = (idx_all[:, :, None] == jnp.arange(N_EXPERTS, dtype=jnp.int32)[None, None, :]).astype(jnp.int32)
    per_tok = jnp.sum(onehot, axis=1)
    before = jnp.cumsum(per_tok, axis=0) - per_tok
    pos = jnp.take_along_axis(before, idx_all, axis=1)
    counts = jnp.sum(per_tok, axis=0)
    padded = (counts + MOE_BM - 1) // MOE_BM * MOE_BM
    pad_end = jnp.cumsum(padded)
    pad_start = pad_end - padded
    dest = (pad_start[idx_all] + pos).astype(jnp.int32).reshape(-1)
    blk_start = jnp.arange(MOE_NB, dtype=jnp.int32) * MOE_BM
    blk_expert = jnp.minimum(jnp.searchsorted(pad_end, blk_start, side="right"), N_EXPERTS - 1).astype(jnp.int32)
    blk_valid = jnp.clip(pad_start[blk_expert] + counts[blk_expert] - blk_start, 0, MOE_BM).astype(jnp.int32)
    n_used = (pad_end[-1] // MOE_BM).astype(jnp.int32).reshape(1)
    return dest, blk_expert, blk_valid, n_used


def _shared_ffn(h, wg, wu, wd):
    hb = h.astype(BF16)
    return _dot((_silu(_dot(hb, wg)) * _dot(hb, wu)).astype(BF16), wd)


def _dispatch_kernel(dest_ref, h2_ref, wg_ref, wu_ref, wd_ref, *refs, aliased):
    if aliased:
        _, xs_ref, sh_ref, sem = refs
    else:
        xs_ref, sh_ref, sem = refs
    tm = h2_ref.shape[0]
    base = pl.program_id(0) * tm

    def issue(t, carry):
        for k in range(TOP_K):
            d = dest_ref[(base + t) * TOP_K + k]
            pltpu.make_async_copy(h2_ref.at[pl.ds(t, 1)], xs_ref.at[pl.ds(d, 1)], sem).start()
        return carry

    lax.fori_loop(0, tm, issue, 0)
    sh_ref[...] = _shared_ffn(h2_ref[...], wg_ref[...], wu_ref[...], wd_ref[...])
    for k in range(TOP_K):
        pltpu.make_async_copy(h2_ref, xs_ref.at[pl.ds(0, tm)], sem).wait()


def _dispatch_call(dest, h2, sh_wg, sh_wu, sh_wd, layer, tm, xs_prev=None):
    m = h2.shape[0]
    aliased = xs_prev is not None
    wspec_in = pl.BlockSpec((None, D_MODEL, D_EXPERT), lambda i, d: (layer, 0, 0))
    wspec_out = pl.BlockSpec((None, D_EXPERT, D_MODEL), lambda i, d: (layer, 0, 0))
    in_specs = [pl.BlockSpec((tm, D_MODEL), lambda i, d: (i, 0)), wspec_in, wspec_in, wspec_out]
    args = [dest, h2, sh_wg, sh_wu, sh_wd]
    if aliased:
        in_specs.append(pl.BlockSpec(memory_space=pl.ANY))
        args.append(xs_prev)
    grid_spec = pltpu.PrefetchScalarGridSpec(
        num_scalar_prefetch=1,
        grid=(m // tm,),
        in_specs=in_specs,
        out_specs=(pl.BlockSpec(memory_space=pl.ANY), pl.BlockSpec((tm, D_MODEL), lambda i, d: (i, 0))),
        scratch_shapes=[pltpu.SemaphoreType.DMA],
    )
    return pl.pallas_call(
        functools.partial(_dispatch_kernel, aliased=aliased),
        out_shape=(jax.ShapeDtypeStruct((MOE_ROWS, D_MODEL), F32), jax.ShapeDtypeStruct((m, D_MODEL), F32)),
        grid_spec=grid_spec,
        input_output_aliases={5: 0} if aliased else {},
        compiler_params=_cparams(("arbitrary",), has_side_effects=True),
        name="moe_dispatch",
    )(*args)


def _expert_kernel(be_ref, nv_ref, nu_ref, x_ref, wg_ref, wu_ref, wd_ref, o_ref, wg_s, wu_s, wd_s):
    i = pl.program_id(0)

    @pl.when(i < nu_ref[0])
    def _():
        prev = be_ref[jnp.maximum(i - 1, 0)]

        @pl.when((i == 0) | (be_ref[i] != prev))
        def _():
            wg_s[...] = wg_ref[...].astype(BF16)
            wu_s[...] = wu_ref[...].astype(BF16)
            wd_s[...] = wd_ref[...].astype(BF16)

        rows = lax.broadcasted_iota(jnp.int32, (MOE_BM, 1), 0)
        x = jnp.where(rows < nv_ref[i], x_ref[...], 0.0).astype(BF16)
        g = _dot(x, wg_s[...])
        u = _dot(x, wu_s[...])
        o_ref[...] = _dot((_silu(g) * u).astype(BF16), wd_s[...])


def _expert_call(blk_expert, blk_valid, n_used, xs, w_gate, w_up, w_down, layer):
    def blk(i, be, nv, nu):
        return jnp.minimum(i, nu[0] - 1)

    def wmap(i, be, nv, nu):
        return (layer, be[blk(i, be, nv, nu)], 0, 0)

    rows = pl.BlockSpec((MOE_BM, D_MODEL), lambda i, be, nv, nu: (blk(i, be, nv, nu), 0))
    grid_spec = pltpu.PrefetchScalarGridSpec(
        num_scalar_prefetch=3,
        grid=(MOE_NB,),
        in_specs=[rows,
                  pl.BlockSpec((None, None, D_MODEL, D_EXPERT), wmap),
                  pl.BlockSpec((None, None, D_MODEL, D_EXPERT), wmap),
                  pl.BlockSpec((None, None, D_EXPERT, D_MODEL), wmap)],
        out_specs=rows,
        scratch_shapes=[pltpu.VMEM((D_MODEL, D_EXPERT), BF16), pltpu.VMEM((D_MODEL, D_EXPERT), BF16),
                        pltpu.VMEM((D_EXPERT, D_MODEL), BF16)],
    )
    return pl.pallas_call(
        _expert_kernel,
        out_shape=jax.ShapeDtypeStruct((MOE_ROWS, D_MODEL), F32),
        grid_spec=grid_spec,
        compiler_params=_cparams(("arbitrary",)),
        name="moe_experts",
    )(blk_expert, blk_valid, n_used, xs, w_gate, w_up, w_down)


def _combine_kernel(dest_ref, rows_ref, gate_ref, sh_ref, x1_ref, g2_ref, lg_ref, lb_ref, o_ref, gbuf, sem):
    tm = x1_ref.shape[0]
    base = pl.program_id(0) * tm

    def issue(t, carry):
        for k in range(TOP_K):
            d = dest_ref[(base + t) * TOP_K + k]
            pltpu.make_async_copy(rows_ref.at[pl.ds(d, 1)], gbuf.at[k, pl.ds(t, 1)], sem).start()
        return carry

    lax.fori_loop(0, tm, issue, 0)
    for k in range(TOP_K):
        pltpu.make_async_copy(rows_ref.at[pl.ds(0, tm)], gbuf.at[k], sem).wait()
    gate = gate_ref[...]
    moe = sh_ref[...]
    for k in range(TOP_K):
        moe = moe + gate[:, k:k + 1] * gbuf[k]
    o_ref[...] = _layer_norm(ALPHA * x1_ref[...] + g2_ref[...] * moe, lg_ref[...], lb_ref[...])


def _combine_call(dest, out_rows, gate, sh, x1, ada, prm, rows_per_b, tm):
    m = x1.shape[0]
    r = ada.shape[1]
    row = pl.BlockSpec((tm, D_MODEL), lambda i, d: (i, 0))
    vec = pl.BlockSpec((1, D_MODEL), lambda i, d: (0, 0))
    grid_spec = pltpu.PrefetchScalarGridSpec(
        num_scalar_prefetch=1,
        grid=(m // tm,),
        in_specs=[pl.BlockSpec(memory_space=pl.ANY),
                  pl.BlockSpec((tm, LANES), lambda i, d: (i, 0)), row, row,
                  pl.BlockSpec((None, r, D_MODEL), _mod_spec(rows_per_b, tm, 5)), vec, vec],
        out_specs=row,
        scratch_shapes=[pltpu.VMEM((TOP_K, tm, D_MODEL), F32), pltpu.SemaphoreType.DMA],
    )
    return pl.pallas_call(
        _combine_kernel,
        out_shape=jax.ShapeDtypeStruct((m, D_MODEL), F32),
        grid_spec=grid_spec,
        compiler_params=_cparams(("arbitrary",)),
        name="moe_combine",
    )(dest, out_rows, gate, sh, x1, ada, prm["ln2_g"], prm["ln2_b"])


def _block_diag(w):
    eye = jnp.eye(LRU_BLOCKS, dtype=w.dtype)
    full = jnp.einsum("hcd,hg->hcgd", w, eye).reshape(W_GROUP, W_GROUP)
    return full.astype(BF16)


def _rope_tables(pos):
    half = HEAD_DIM // 2
    freqs = ROPE_THETA ** (-jnp.arange(half, dtype=F32) / half)
    ang = pos.astype(F32)[:, None] * freqs[None, :]
    cos = jnp.cos(ang)
    sin = jnp.sin(ang)
    cos_t = jnp.concatenate([cos, cos, cos, cos], axis=1)
    sin_t = jnp.concatenate([-sin, sin, -sin, sin], axis=1)
    return cos_t, sin_t


def kernel(x_prompt, x_sample, cache_k, cache_v, state_conv_a, state_lru, state_conv_c, state_conv_d, page_table, c_prompt, c_sample, w_ada, b_ada, w_in, conv_a_w, conv_a_b, lru_wr, lru_br, lru_wi, lru_bi, lru_lambda, lam_q1, lam_k1, lam_q2, lam_k2, subln_g, conv_c_w, conv_c_b, ln_c_g, ln_c_b, conv_d_w, w_out, ln1_g, ln1_b, ln2_g, ln2_b, router_w, router_bias, exp_w_gate, exp_w_up, exp_w_down, sh_w_gate, sh_w_up, sh_w_down):
    n_pool = cache_k.shape[1]
    c_all = jnp.concatenate([c_prompt, c_sample, jnp.zeros((16 - BATCH - DEC_BATCH, D_MODEL), F32)], axis=0)
    ada_all = _ada_call(c_all, w_ada, b_ada)
    w_in_bf = w_in.astype(BF16)
    w_out_bf = w_out.astype(BF16)
    sh_wg_bf = sh_w_gate.astype(BF16)
    sh_wu_bf = sh_w_up.astype(BF16)
    sh_wd_bf = sh_w_down.astype(BF16)
    cache_k4 = cache_k.reshape(DEPTH, n_pool, PAGE_SIZE * N_HEADS, LANES)
    cache_v4 = cache_v.reshape(DEPTH, n_pool, PAGE_SIZE * N_HEADS, LANES)
    cos_p, sin_p = _rope_tables(jnp.arange(SEQ, dtype=jnp.int32))
    cos_s, sin_s = _rope_tables(jnp.full((1,), PAST_LEN, jnp.int32))
    lane = jnp.arange(LANES)
    comp_mask = jnp.stack([lane < HEAD_DIM, lane >= HEAD_DIM]).astype(F32)

    xp = x_prompt.reshape(T_PROMPT, D_MODEL)
    xs = x_sample.reshape(DEC_BATCH, D_MODEL)
    outs = [[] for _ in range(12)]
    for l in range(DEPTH):
        lam_init = 0.8 - 0.6 * math.exp(-0.3 * l)
        prm = dict(
            conv_a_w=conv_a_w[l], conv_a_b=conv_a_b[l][None], wr=_block_diag(lru_wr[l]), br=lru_br[l][None],
            wi=_block_diag(lru_wi[l]), bi=lru_bi[l][None], lam=lru_lambda[l][None],
            conv_c_w=conv_c_w[l], conv_c_b=conv_c_b[l][None], ln_c_g=ln_c_g[l][None], ln_c_b=ln_c_b[l][None],
            conv_d_w=conv_d_w[l],
            lam_q1=lam_q1[l][None], lam_k1=lam_k1[l][None], lam_q2=lam_q2[l][None], lam_k2=lam_k2[l][None],
            subln_g=subln_g[l][None], lam_init=jnp.full((1, 1), lam_init, F32),
            ln1_g=ln1_g[l][None], ln1_b=ln1_b[l][None], ln2_g=ln2_g[l][None], ln2_b=ln2_b[l][None],
            router_w=router_w[l], router_bias=router_bias[l][None],
        )
        ada_p = ada_all[l, :BATCH].reshape(BATCH, 1, 6 * D_MODEL)
        ada_s = ada_all[l, BATCH:BATCH + DEC_BATCH].reshape(1, DEC_BATCH, 6 * D_MODEL)

        proj_p = _inproj_call(xp, ada_p, w_in_bf, l, SEQ, 1024, 512).reshape(BATCH, SEQ, D_IN)
        ya, yc, yd, q_p, k_p, sa, sh, sc, sd = _mixer_prompt_call(proj_p, cos_p, sin_p, prm)
        yb = _flash_call(q_p, k_p, proj_p, prm)
        outs[0].append(k_p)
        outs[1].append(proj_p[:, :, 4 * W_GROUP:5 * W_GROUP])
        outs[2].append(sa)
        outs[3].append(sh.reshape(BATCH, W_GROUP))
        outs[4].append(sc)
        outs[5].append(sd)

        proj_s = _inproj_call(xs, ada_s, w_in_bf, l, DEC_BATCH, DEC_BATCH, 512)
        ya_s, yc_s, yd_s, q_s, k_s, sa_s, sh_s, sc_s, sd_s = _mixer_sample_call(
            proj_s, cos_s, sin_s, prm,
            state_conv_a[l].transpose(1, 0, 2), state_lru[l], state_conv_c[l].transpose(1, 0, 2),
            state_conv_d[l].transpose(1, 0, 2))
        v_s = proj_s[:, 4 * W_GROUP:5 * W_GROUP]
        q4 = q_s.reshape(DEC_BATCH, 1, N_HEADS, LANES) * (HEAD_DIM ** -0.5)
        q8 = (q4 * comp_mask[None, :, None, :]).reshape(DEC_BATCH, 2 * N_HEADS, LANES).astype(BF16)
        pad4 = jnp.zeros((DEC_BATCH, N_HEADS, LANES), F32)
        k_new8 = jnp.concatenate([k_s.reshape(DEC_BATCH, N_HEADS, LANES), pad4], axis=1)
        v_new8 = jnp.concatenate([v_s.reshape(DEC_BATCH, N_HEADS, LANES), pad4], axis=1)
        yb_s = _decode_call(q8, k_new8, v_new8, cache_k4, cache_v4, page_table, prm, l)
        yb_s = yb_s.reshape(DEC_BATCH, W_GROUP).astype(BF16)
        outs[6].append(k_s.reshape(DEC_BATCH, 1, N_HEADS, 2 * HEAD_DIM))
        outs[7].append(v_s.reshape(DEC_BATCH, 1, N_HEADS, 2 * HEAD_DIM))
        outs[8].append(sa_s.transpose(1, 0, 2))
        outs[9].append(sh_s)
        outs[10].append(sc_s.transpose(1, 0, 2))
        outs[11].append(sd_s.transpose(1, 0, 2))

        flat = lambda t: t.reshape(T_PROMPT, W_GROUP)
        x1_p, h2_p, idx_p, gate_p = _outproj_call(flat(ya), flat(yb), flat(yc), flat(yd), w_out_bf, xp, ada_p, prm, l, SEQ, 256)
        x1_s, h2_s, idx_s, gate_s = _outproj_call(ya_s, yb_s, yc_s, yd_s, w_out_bf, xs, ada_s, prm, l, DEC_BATCH, DEC_BATCH)

        idx_all = jnp.concatenate([idx_p[:, :TOP_K], idx_s[:, :TOP_K]], axis=0)
        dest, blk_expert, blk_valid, n_used = _moe_plan(idx_all)
        dest_p, dest_s = dest[:T_PROMPT * TOP_K], dest[T_PROMPT * TOP_K:]
        x_sorted, shared_p = _dispatch_call(dest_p, h2_p, sh_wg_bf, sh_wu_bf, sh_wd_bf, l, 256)
        x_sorted, shared_s = _dispatch_call(dest_s, h2_s, sh_wg_bf, sh_wu_bf, sh_wd_bf, l, DEC_BATCH, xs_prev=x_sorted)
        out_rows = _expert_call(blk_expert, blk_valid, n_used, x_sorted, exp_w_gate, exp_w_up, exp_w_down, l)
        xp = _combine_call(dest_p, out_rows, gate_p, shared_p, x1_p, ada_p, prm, SEQ, 128)
        xs = _combine_call(dest_s, out_rows, gate_s, shared_s, x1_s, ada_s, prm, DEC_BATCH, DEC_BATCH)

    st = [jnp.stack(o, axis=0) for o in outs]
    kv_shape = (DEPTH, BATCH, SEQ // PAGE_SIZE, PAGE_SIZE, N_HEADS, 2 * HEAD_DIM)
    return (xp.reshape(BATCH, SEQ, D_MODEL), xs.reshape(DEC_BATCH, 1, D_MODEL),
            st[0].reshape(kv_shape), st[1].reshape(kv_shape), st[2], st[3], st[4], st[5],
            st[6], st[7], st[8], st[9], st[10], st[11])
```

```python
import functools
import math

import jax
import jax.numpy as jnp
from jax import lax
from jax.experimental import pallas as pl
from jax.experimental.pallas import tpu as pltpu

F32 = jnp.float32
BF16 = jnp.bfloat16

D_MODEL = 2048
BATCH = 2
SEQ = 4096
DEPTH = 4
DEC_BATCH = 8
PAST_LEN = 16384
PAGE_SIZE = 128
W_GROUP = 512
N_IN_PARTS = 10
D_IN = N_IN_PARTS * W_GROUP
CONV_A = 4
LRU_BLOCKS = 8
LRU_BLOCK_DIM = W_GROUP // LRU_BLOCKS
LRU_C = 8.0
HEAD_DIM = 64
N_HEADS = 4
ROPE_THETA = 10000.0
CONV_C = 31
CONV_D = 3
N_EXPERTS = 64
TOP_K = 8
D_EXPERT = 512
ROUTE_SCALE = 2.5
LN_EPS = 1e-5
ALPHA = (2.0 * DEPTH) ** 0.25
NEG_INF = -1e30

LANES = 128
VMEM_LIMIT = 56 * 1024 * 1024

T_PROMPT = BATCH * SEQ
T_ALL = T_PROMPT + DEC_BATCH
MOE_BM = 256
MOE_NB = -(-T_ALL * TOP_K // MOE_BM) + N_EXPERTS
MOE_ROWS = MOE_NB * MOE_BM
N_PAGES = PAST_LEN // PAGE_SIZE
DEC_G = 8


def _cparams(sem, **kw):
    return pltpu.CompilerParams(dimension_semantics=sem, vmem_limit_bytes=VMEM_LIMIT, **kw)


def _dot(a, b):
    return jnp.dot(a, b, preferred_element_type=F32)


def _dot_nt(a, b):
    return lax.dot_general(a, b, (((1,), (1,)), ((), ())), preferred_element_type=F32)


def _split_bf16(x):
    hi = x.astype(BF16)
    lo = (x - hi.astype(F32)).astype(BF16)
    return hi, lo


def _dot3(a, b):
    ah, al = _split_bf16(a)
    bh, bl = _split_bf16(b)
    return _dot(ah, bh) + _dot(ah, bl) + _dot(al, bh)


def _sigmoid(x):
    return 1.0 / (1.0 + jnp.exp(-x))


def _silu(x):
    return x * _sigmoid(x)


def _layer_norm(z, g, b):
    mu = jnp.mean(z, axis=-1, keepdims=True)
    zc = z - mu
    var = jnp.mean(zc * zc, axis=-1, keepdims=True)
    return zc * lax.rsqrt(var + LN_EPS) * g + b


def _ada_kernel(c_ref, w_ref, b_ref, o_ref):
    c = c_ref[...]
    o_ref[...] = _dot3(_silu(c), w_ref[...]) + b_ref[...]


def _ada_call(c_all, w_ada, b_ada):
    r = c_all.shape[0]
    tn = 1024
    return pl.pallas_call(
        _ada_kernel,
        out_shape=jax.ShapeDtypeStruct((DEPTH, r, 6 * D_MODEL), F32),
        grid=(DEPTH, 6 * D_MODEL // tn),
        in_specs=[
            pl.BlockSpec((r, D_MODEL), lambda l, n: (0, 0)),
            pl.BlockSpec((None, D_MODEL, tn), lambda l, n: (l, 0, n)),
            pl.BlockSpec((None, 1, tn), lambda l, n: (l, 0, n)),
        ],
        out_specs=pl.BlockSpec((None, r, tn), lambda l, n: (l, 0, n)),
        compiler_params=_cparams(("arbitrary", "arbitrary")),
        name="ada",
    )(c_all, w_ada, b_ada.reshape(DEPTH, 1, 6 * D_MODEL))


def _inproj_kernel(x_ref, sc_ref, sh_ref, w_ref, o_ref, h_scr):
    @pl.when(pl.program_id(1) == 0)
    def _():
        h_scr[...] = (x_ref[...] * (1.0 + sc_ref[...]) + sh_ref[...]).astype(BF16)

    o_ref[...] = _dot(h_scr[...], w_ref[...])


def _mod_spec(rows_per_b, tm, col):
    def imap(m, *_):
        return ((m * tm) // rows_per_b, 0, col)
    return imap


def _inproj_call(x, ada, w_in_bf, layer, rows_per_b, tm, tn):
    m = x.shape[0]
    r = ada.shape[1]
    return pl.pallas_call(
        _inproj_kernel,
        out_shape=jax.ShapeDtypeStruct((m, D_IN), F32),
        grid=(m // tm, D_IN // tn),
        in_specs=[
            pl.BlockSpec((tm, D_MODEL), lambda i, n: (i, 0)),
            pl.BlockSpec((None, r, D_MODEL), _mod_spec(rows_per_b, tm, 1)),
            pl.BlockSpec((None, r, D_MODEL), _mod_spec(rows_per_b, tm, 0)),
            pl.BlockSpec((None, D_MODEL, tn), lambda i, n: (layer, 0, n)),
        ],
        out_specs=pl.BlockSpec((tm, tn), lambda i, n: (i, n)),
        scratch_shapes=[pltpu.VMEM((tm, D_MODEL), BF16)],
        compiler_params=_cparams(("arbitrary", "arbitrary")),
        name="inproj",
    )(x, ada, ada, w_in_bf)


def _softplus(z):
    return jnp.maximum(z, 0.0) + jnp.log1p(jnp.exp(-jnp.abs(z)))


def _gelu_tanh(x):
    return 0.5 * x * (1.0 + jnp.tanh(math.sqrt(2.0 / math.pi) * (x + 0.044715 * (x * x * x))))


def _lru_gates(xa, wr, br, wi, bi, lam):
    xb = xa.astype(BF16)
    r = _sigmoid(_dot(xb, wr) + br)
    i = _sigmoid(_dot(xb, wi) + bi)
    log_a = -LRU_C * r * _softplus(-lam)
    a = jnp.exp(log_a)
    u = jnp.sqrt(jnp.tanh(-log_a) * (1.0 + a * a)) * (i * xa)
    return a, u


def _rope(x, cos_t, sin_t):
    lane = lax.broadcasted_iota(jnp.int32, x.shape, 1)
    first_half = (lane & (HEAD_DIM // 2)) == 0
    w = x.shape[1]
    partner = jnp.where(first_half, pltpu.roll(x, w - HEAD_DIM // 2, axis=1), pltpu.roll(x, HEAD_DIM // 2, axis=1))
    return x * cos_t + partner * sin_t


def _tile4(t):
    return jnp.concatenate([t, t, t, t], axis=1)


def _conv_from_buf(buf_ref, w_ref, hist, n_taps, rows, rb):
    outs = []
    for r0 in range(0, rows, rb):
        acc = None
        for j in range(n_taps):
            term = buf_ref[pl.ds(hist - (n_taps - 1) + j + r0, rb), :] * w_ref[pl.ds(j, 1), :]
            acc = term if acc is None else acc + term
        outs.append(acc)
    return jnp.concatenate(outs, axis=0) if len(outs) > 1 else outs[0]


def _linear_scan(a, b):
    n = a.shape[0]
    rows = lax.broadcasted_iota(jnp.int32, a.shape, 0)
    d = 1
    while d < n:
        keep = rows >= d
        a_sh = jnp.where(keep, pltpu.roll(a, d, axis=0), 1.0)
        b_sh = jnp.where(keep, pltpu.roll(b, d, axis=0), 0.0)
        b = a * b_sh + b
        a = a * a_sh
        d *= 2
    return a, b


MIX_TC = 256
HIST_A = 8
HIST_C = 32
HIST_D = 8
CONV_RB = 32


def _mixer_prompt_kernel(ax_ref, ag_ref, bq_ref, bk_ref, cv_ref, cg_ref, db_ref, dc_ref, dx_ref,
                         cos_ref, sin_ref, caw_ref, cab_ref, wr_ref, br_ref, wi_ref, bi_ref, lam_ref,
                         ccw_ref, ccb_ref, lng_ref, lnb_ref, cdw_ref,
                         ya_ref, yc_ref, yd_ref, q_ref, k_ref, sa_ref, sh_ref, sc_ref, sd_ref,
                         xa_buf, uc_buf, dd_buf, h_buf):
    tc = MIX_TC

    @pl.when(pl.program_id(1) == 0)
    def _():
        xa_buf[pl.ds(0, HIST_A), :] = jnp.zeros((HIST_A, W_GROUP), F32)
        uc_buf[pl.ds(0, HIST_C), :] = jnp.zeros((HIST_C, W_GROUP), F32)
        dd_buf[pl.ds(0, HIST_D), :] = jnp.zeros((HIST_D, W_GROUP), F32)
        h_buf[...] = jnp.zeros((1, W_GROUP), F32)

    xa_buf[pl.ds(HIST_A, tc), :] = ax_ref[...]
    xa = _conv_from_buf(xa_buf, caw_ref, HIST_A, CONV_A, tc, CONV_RB) + cab_ref[...]
    a, u = _lru_gates(xa, wr_ref[...], br_ref[...], wi_ref[...], bi_ref[...], lam_ref[...])
    a_cum, b_cum = _linear_scan(a, u)
    h = a_cum * h_buf[...] + b_cum
    ya_ref[...] = (_gelu_tanh(ag_ref[...]) * h).astype(BF16)
    h_last = h[tc - 1:tc, :]
    h_buf[...] = h_last
    sh_ref[...] = h_last
    sa_ref[...] = xa_buf[pl.ds(HIST_A + tc - (CONV_A - 1), CONV_A - 1), :]
    xa_buf[pl.ds(0, HIST_A), :] = xa_buf[pl.ds(tc, HIST_A), :]

    uc_buf[pl.ds(HIST_C, tc), :] = cv_ref[...] * _sigmoid(cg_ref[...])
    yc = _conv_from_buf(uc_buf, ccw_ref, HIST_C, CONV_C, tc, CONV_RB) + ccb_ref[...]
    yc_ref[...] = _silu(_layer_norm(yc, lng_ref[...], lnb_ref[...])).astype(BF16)
    sc_ref[...] = uc_buf[pl.ds(HIST_C + tc - (CONV_C - 1), CONV_C - 1), :]
    uc_buf[pl.ds(0, HIST_C), :] = uc_buf[pl.ds(tc, HIST_C), :]

    dd_buf[pl.ds(HIST_D, tc), :] = dc_ref[...] * dx_ref[...]
    yd = _conv_from_buf(dd_buf, cdw_ref, HIST_D, CONV_D, tc, CONV_RB)
    yd_ref[...] = (db_ref[...] * yd).astype(BF16)
    sd_ref[...] = dd_buf[pl.ds(HIST_D + tc - (CONV_D - 1), CONV_D - 1), :]
    dd_buf[pl.ds(0, HIST_D), :] = dd_buf[pl.ds(tc, HIST_D), :]

    cos_t = _tile4(cos_ref[...])
    sin_t = _tile4(sin_ref[...])
    q_ref[...] = (_rope(bq_ref[...], cos_t, sin_t) * (HEAD_DIM ** -0.5)).astype(BF16)
    k_ref[...] = _rope(bk_ref[...], cos_t, sin_t)


def _mixer_prompt_call(proj, cos_t, sin_t, prm):
    tc = MIX_TC

    def col(j):
        return pl.BlockSpec((None, tc, W_GROUP), lambda b, c: (b, c, j))

    def full(arr):
        nd = arr.ndim
        return pl.BlockSpec(arr.shape, lambda b, c: (0,) * nd)

    small = [prm["conv_a_w"], prm["conv_a_b"], prm["wr"], prm["br"], prm["wi"], prm["bi"], prm["lam"],
             prm["conv_c_w"], prm["conv_c_b"], prm["ln_c_g"], prm["ln_c_b"], prm["conv_d_w"]]
    tok = pl.BlockSpec((None, tc, W_GROUP), lambda b, c: (b, c, 0))

    def state(n):
        return pl.BlockSpec((None, n, W_GROUP), lambda b, c: (b, 0, 0))

    out_shape = (
        jax.ShapeDtypeStruct((BATCH, SEQ, W_GROUP), BF16),
        jax.ShapeDtypeStruct((BATCH, SEQ, W_GROUP), BF16),
        jax.ShapeDtypeStruct((BATCH, SEQ, W_GROUP), BF16),
        jax.ShapeDtypeStruct((BATCH, SEQ, W_GROUP), BF16),
        jax.ShapeDtypeStruct((BATCH, SEQ, W_GROUP), F32),
        jax.ShapeDtypeStruct((BATCH, CONV_A - 1, W_GROUP), F32),
        jax.ShapeDtypeStruct((BATCH, 1, W_GROUP), F32),
        jax.ShapeDtypeStruct((BATCH, CONV_C - 1, W_GROUP), F32),
        jax.ShapeDtypeStruct((BATCH, CONV_D - 1, W_GROUP), F32),
    )
    return pl.pallas_call(
        _mixer_prompt_kernel,
        out_shape=out_shape,
        grid=(BATCH, SEQ // tc),
        in_specs=[col(0), col(1), col(2), col(3), col(5), col(6), col(7), col(8), col(9),
                  pl.BlockSpec((tc, LANES), lambda b, c: (c, 0)),
                  pl.BlockSpec((tc, LANES), lambda b, c: (c, 0))] + [full(a) for a in small],
        out_specs=(tok, tok, tok, tok, tok, state(CONV_A - 1), state(1), state(CONV_C - 1), state(CONV_D - 1)),
        scratch_shapes=[pltpu.VMEM((HIST_A + tc, W_GROUP), F32), pltpu.VMEM((HIST_C + tc, W_GROUP), F32),
                        pltpu.VMEM((HIST_D + tc, W_GROUP), F32), pltpu.VMEM((1, W_GROUP), F32)],
        compiler_params=_cparams(("arbitrary", "arbitrary")),
        name="mixer_prompt",
    )(*([proj] * 9), cos_t, sin_t, *small)


def _mixer_sample_kernel(ax_ref, ag_ref, bq_ref, bk_ref, cv_ref, cg_ref, db_ref, dc_ref, dx_ref,
                         cos_ref, sin_ref, caw_ref, cab_ref, wr_ref, br_ref, wi_ref, bi_ref, lam_ref,
                         ccw_ref, ccb_ref, lng_ref, lnb_ref, cdw_ref,
                         ba_ref, h0_ref, bc_ref, bd_ref,
                         ya_ref, yc_ref, yd_ref, q_ref, k_ref, sa_ref, sh_ref, sc_ref, sd_ref):
    ax = ax_ref[...]
    xa = cab_ref[...] + caw_ref[pl.ds(CONV_A - 1, 1), :] * ax
    for j in range(CONV_A - 1):
        xa = xa + caw_ref[pl.ds(j, 1), :] * ba_ref[j]
    a, u = _lru_gates(xa, wr_ref[...], br_ref[...], wi_ref[...], bi_ref[...], lam_ref[...])
    h = a * h0_ref[...] + u
    ya_ref[...] = (_gelu_tanh(ag_ref[...]) * h).astype(BF16)
    sh_ref[...] = h
    for j in range(CONV_A - 2):
        sa_ref[j] = ba_ref[j + 1]
    sa_ref[CONV_A - 2] = ax

    uc = cv_ref[...] * _sigmoid(cg_ref[...])
    yc = ccb_ref[...] + ccw_ref[pl.ds(CONV_C - 1, 1), :] * uc
    for j in range(CONV_C - 1):
        yc = yc + ccw_ref[pl.ds(j, 1), :] * bc_ref[j]
    yc_ref[...] = _silu(_layer_norm(yc, lng_ref[...], lnb_ref[...])).astype(BF16)
    for j in range(CONV_C - 2):
        sc_ref[j] = bc_ref[j + 1]
    sc_ref[CONV_C - 2] = uc

    dd = dc_ref[...] * dx_ref[...]
    yd = cdw_ref[pl.ds(CONV_D - 1, 1), :] * dd
    for j in range(CONV_D - 1):
        yd = yd + cdw_ref[pl.ds(j, 1), :] * bd_ref[j]
    yd_ref[...] = (db_ref[...] * yd).astype(BF16)
    for j in range(CONV_D - 2):
        sd_ref[j] = bd_ref[j + 1]
    sd_ref[CONV_D - 2] = dd

    cos_t = _tile4(cos_ref[...])
    sin_t = _tile4(sin_ref[...])
    q_ref[...] = _rope(bq_ref[...], cos_t, sin_t)
    k_ref[...] = _rope(bk_ref[...], cos_t, sin_t)


def _mixer_sample_call(proj, cos_t, sin_t, prm, buf_a, h0, buf_c, buf_d):
    nb = DEC_BATCH

    def col(j):
        return pl.BlockSpec((nb, W_GROUP), lambda i: (0, j))

    def full(arr):
        nd = arr.ndim
        return pl.BlockSpec(arr.shape, lambda i: (0,) * nd)

    small = [prm["conv_a_w"], prm["conv_a_b"], prm["wr"], prm["br"], prm["wi"], prm["bi"], prm["lam"],
             prm["conv_c_w"], prm["conv_c_b"], prm["ln_c_g"], prm["ln_c_b"], prm["conv_d_w"]]
    states = [buf_a, h0, buf_c, buf_d]
    tok_bf = jax.ShapeDtypeStruct((nb, W_GROUP), BF16)
    tok_f = jax.ShapeDtypeStruct((nb, W_GROUP), F32)
    out_shape = (tok_bf, tok_bf, tok_bf, tok_f, tok_f,
                 jax.ShapeDtypeStruct(buf_a.shape, F32), tok_f,
                 jax.ShapeDtypeStruct(buf_c.shape, F32), jax.ShapeDtypeStruct(buf_d.shape, F32))
    return pl.pallas_call(
        _mixer_sample_kernel,
        out_shape=out_shape,
        grid=(1,),
        in_specs=[col(0), col(1), col(2), col(3), col(5), col(6), col(7), col(8), col(9),
                  full(cos_t), full(sin_t)] + [full(a) for a in small] + [full(a) for a in states],
        out_specs=tuple(pl.BlockSpec(s.shape, lambda i, nd=len(s.shape): (0,) * nd) for s in out_shape),
        compiler_params=_cparams(("arbitrary",)),
        name="mixer_sample",
    )(*([proj] * 9), cos_t, sin_t, *small, *states)


def _diff_lambda(lq1_ref, lk1_ref, lq2_ref, lk2_ref, lam_init):
    s1 = jnp.sum(lq1_ref[...] * lk1_ref[...], axis=-1, keepdims=True)
    s2 = jnp.sum(lq2_ref[...] * lk2_ref[...], axis=-1, keepdims=True)
    return jnp.exp(s1) - jnp.exp(s2) + lam_init


def _diff_finish(o1, o2, lam, g, lam_init):
    od = o1 - lam * o2
    ms = jnp.mean(od * od, axis=-1, keepdims=True)
    return od * lax.rsqrt(ms + LN_EPS) * g * (1.0 - lam_init)


FLASH_T = 512


def _flash_kernel(q_ref, k_ref, v_ref, lq1_ref, lk1_ref, lq2_ref, lk2_ref, gcol_ref, li_ref, o_ref,
                  kb_scr, vt_scr, m_scr, l_scr, acc_scr):
    t = FLASH_T
    i = pl.program_id(2)

    @pl.when(i == 0)
    def _():
        for c in range(SEQ // t):
            kb_scr[c] = k_ref[pl.ds(c * t, t), :].astype(BF16)
            vt_scr[c] = v_ref[pl.ds(c * t, t), :].T.astype(BF16)

    q = q_ref[...]
    lane = lax.broadcasted_iota(jnp.int32, q.shape, 1)
    zero = jnp.zeros_like(q)
    qq = jnp.concatenate([jnp.where(lane < HEAD_DIM, q, zero), jnp.where(lane >= HEAD_DIM, q, zero)], axis=0)
    m_scr[...] = jnp.full(m_scr.shape, NEG_INF, F32)
    l_scr[...] = jnp.zeros(l_scr.shape, F32)
    acc_scr[...] = jnp.zeros(acc_scr.shape, F32)

    def step(j, masked):
        st = _dot_nt(kb_scr[j], qq)
        if masked:
            r = lax.broadcasted_iota(jnp.int32, st.shape, 0)
            c = lax.broadcasted_iota(jnp.int32, st.shape, 1)
            st = jnp.where(r <= jnp.where(c >= t, c - t, c), st, NEG_INF)
        m_prev = m_scr[...]
        m_new = jnp.maximum(m_prev, jnp.max(st, axis=0, keepdims=True))
        alpha = jnp.exp(m_prev - m_new)
        p = jnp.exp(st - m_new)
        l_scr[...] = alpha * l_scr[...] + jnp.sum(p, axis=0, keepdims=True)
        acc_scr[...] = alpha * acc_scr[...] + _dot(vt_scr[j], p.astype(BF16))
        m_scr[...] = m_new

    def body(j, carry):
        step(j, False)
        return carry

    lax.fori_loop(0, i, body, 0)
    step(i, True)

    o = acc_scr[...] / l_scr[...]
    lam_init = li_ref[...]
    lam = _diff_lambda(lq1_ref, lk1_ref, lq2_ref, lk2_ref, lam_init)
    od = o[:, :t] - lam * o[:, t:]
    ms = jnp.mean(od * od, axis=0, keepdims=True)
    y = od * lax.rsqrt(ms + LN_EPS) * gcol_ref[...] * (1.0 - lam_init)
    o_ref[...] = y.T.astype(BF16)


def _flash_call(q, k, proj, prm):
    t = FLASH_T
    v_col0 = 4 * W_GROUP // LANES

    def full(arr):
        nd = arr.ndim
        return pl.BlockSpec(arr.shape, lambda b, h, i: (0,) * nd)

    small = [prm["lam_q1"], prm["lam_k1"], prm["lam_q2"], prm["lam_k2"], prm["subln_g_col"], prm["lam_init"]]
    return pl.pallas_call(
        _flash_kernel,
        out_shape=jax.ShapeDtypeStruct((BATCH, SEQ, W_GROUP), BF16),
        grid=(BATCH, N_HEADS, SEQ // t),
        in_specs=[pl.BlockSpec((None, t, LANES), lambda b, h, i: (b, i, h)),
                  pl.BlockSpec((None, SEQ, LANES), lambda b, h, i: (b, 0, h)),
                  pl.BlockSpec((None, SEQ, LANES), lambda b, h, i: (b, 0, v_col0 + h))] + [full(a) for a in small],
        out_specs=pl.BlockSpec((None, t, LANES), lambda b, h, i: (b, i, h)),
        scratch_shapes=[pltpu.VMEM((SEQ // t, t, LANES), BF16), pltpu.VMEM((SEQ // t, LANES, t), BF16),
                        pltpu.VMEM((1, 2 * t), F32), pltpu.VMEM((1, 2 * t), F32), pltpu.VMEM((LANES, 2 * t), F32)],
        compiler_params=_cparams(("arbitrary", "arbitrary", "arbitrary")),
        name="flash",
    )(q, k, proj, *small)


def _softmax_update(s, v, m_scr, l_scr, acc_scr):
    m_prev = m_scr[...]
    m_new = jnp.maximum(m_prev, jnp.max(s, axis=1, keepdims=True))
    alpha = jnp.exp(m_prev - m_new)
    p = jnp.exp(s - m_new)
    l_scr[...] = alpha * l_scr[...] + jnp.sum(p, axis=1, keepdims=True)
    acc_scr[...] = alpha * acc_scr[...] + _dot(p.astype(BF16), v)
    m_scr[...] = m_new


def _decode_kernel(pt_ref, q_ref, *refs):
    k_refs = refs[:DEC_G]
    v_refs = refs[DEC_G:2 * DEC_G]
    (kn_ref, vn_ref, lq1_ref, lk1_ref, lq2_ref, lk2_ref, g_ref, li_ref, o_ref, m_scr, l_scr, acc_scr) = refs[2 * DEC_G:]
    s_idx = pl.program_id(1)

    @pl.when(s_idx == 0)
    def _():
        m_scr[...] = jnp.full(m_scr.shape, NEG_INF, F32)
        l_scr[...] = jnp.zeros(l_scr.shape, F32)
        acc_scr[...] = jnp.zeros(acc_scr.shape, F32)

    q = q_ref[...]
    n_kv = PAGE_SIZE * N_HEADS
    row = lax.broadcasted_iota(jnp.int32, (2 * N_HEADS, n_kv), 0)
    col = lax.broadcasted_iota(jnp.int32, (2 * N_HEADS, n_kv), 1)
    same_head = (col & (N_HEADS - 1)) == (row & (N_HEADS - 1))
    s = jnp.concatenate([jnp.where(same_head, _dot_nt(q, k_refs[g][...].astype(BF16)), NEG_INF)
                         for g in range(DEC_G)], axis=1)
    m_prev = m_scr[...]
    m_new = jnp.maximum(m_prev, jnp.max(s, axis=1, keepdims=True))
    alpha = jnp.exp(m_prev - m_new)
    p = jnp.exp(s - m_new)
    l_scr[...] = alpha * l_scr[...] + jnp.sum(p, axis=1, keepdims=True)
    pb = p.astype(BF16)
    pv = _dot(pb[:, :n_kv], v_refs[0][...].astype(BF16))
    for g in range(1, DEC_G):
        pv = pv + _dot(pb[:, g * n_kv:(g + 1) * n_kv], v_refs[g][...].astype(BF16))
    acc_scr[...] = alpha * acc_scr[...] + pv
    m_scr[...] = m_new

    @pl.when(s_idx == pl.num_programs(1) - 1)
    def _():
        kn = kn_ref[...].astype(BF16)
        vn = vn_ref[...].astype(BF16)
        r8 = lax.broadcasted_iota(jnp.int32, (2 * N_HEADS, 2 * N_HEADS), 0)
        c8 = lax.broadcasted_iota(jnp.int32, (2 * N_HEADS, 2 * N_HEADS), 1)
        s = jnp.where(c8 == (r8 & (N_HEADS - 1)), _dot_nt(q, kn), NEG_INF)
        _softmax_update(s, vn, m_scr, l_scr, acc_scr)
        o = acc_scr[...] / l_scr[...]
        lam_init = li_ref[...]
        lam = _diff_lambda(lq1_ref, lk1_ref, lq2_ref, lk2_ref, lam_init)
        o_ref[...] = _diff_finish(o[:N_HEADS], o[N_HEADS:], lam, g_ref[...], lam_init)


def _decode_call(q8, k_new8, v_new8, cache_k4, cache_v4, page_table, prm, layer):
    n_kv = PAGE_SIZE * N_HEADS
    steps = N_PAGES // DEC_G

    def page_spec(g):
        return pl.BlockSpec((None, None, n_kv, LANES), lambda b, s, pt: (layer, pt[b, s * DEC_G + g], 0, 0))

    def full(arr):
        nd = arr.ndim
        return pl.BlockSpec(arr.shape, lambda b, s, pt: (0,) * nd)

    row8 = pl.BlockSpec((None, 2 * N_HEADS, LANES), lambda b, s, pt: (b, 0, 0))
    small = [prm["lam_q1"], prm["lam_k1"], prm["lam_q2"], prm["lam_k2"], prm["subln_g"], prm["lam_init"]]
    grid_spec = pltpu.PrefetchScalarGridSpec(
        num_scalar_prefetch=1,
        grid=(DEC_BATCH, steps),
        in_specs=[row8] + [page_spec(g) for g in range(DEC_G)] + [page_spec(g) for g in range(DEC_G)]
                 + [row8, row8] + [full(a) for a in small],
        out_specs=pl.BlockSpec((None, N_HEADS, LANES), lambda b, s, pt: (b, 0, 0)),
        scratch_shapes=[pltpu.VMEM((2 * N_HEADS, 1), F32), pltpu.VMEM((2 * N_HEADS, 1), F32),
                        pltpu.VMEM((2 * N_HEADS, LANES), F32)],
    )
    return pl.pallas_call(
        _decode_kernel,
        out_shape=jax.ShapeDtypeStruct((DEC_BATCH, N_HEADS, LANES), F32),
        grid_spec=grid_spec,
        compiler_params=_cparams(("arbitrary", "arbitrary")),
        name="decode_attn",
    )(page_table, q8, *([cache_k4] * DEC_G), *([cache_v4] * DEC_G), k_new8, v_new8, *small)


def _outproj_kernel(ya_ref, yb_ref, yc_ref, yd_ref, w_ref, x_ref, g1_ref, sc2_ref, sh2_ref, lg_ref, lb_ref,
                    rw_ref, rb_ref, x1_ref, h2_ref, idx_ref, gate_ref):
    mix = (_dot(ya_ref[...], w_ref[pl.ds(0, W_GROUP), :]) + _dot(yb_ref[...], w_ref[pl.ds(W_GROUP, W_GROUP), :])
           + _dot(yc_ref[...], w_ref[pl.ds(2 * W_GROUP, W_GROUP), :]) + _dot(yd_ref[...], w_ref[pl.ds(3 * W_GROUP, W_GROUP), :]))
    x1 = _layer_norm(ALPHA * x_ref[...] + g1_ref[...] * mix, lg_ref[...], lb_ref[...])
    x1_ref[...] = x1
    h2 = x1 * (1.0 + sc2_ref[...]) + sh2_ref[...]
    h2_ref[...] = h2

    scores = _sigmoid(_dot3(h2, rw_ref[...]))
    sel = scores + rb_ref[...]
    tm = scores.shape[0]
    lane_e = lax.broadcasted_iota(jnp.int32, (tm, N_EXPERTS), 1).astype(F32)
    lane_o = lax.broadcasted_iota(jnp.int32, (tm, LANES), 1)
    idx_acc = jnp.zeros((tm, LANES), F32)
    gate_acc = jnp.zeros((tm, LANES), F32)
    gsum = jnp.zeros((tm, 1), F32)
    for k in range(TOP_K):
        best = jnp.max(sel, axis=1, keepdims=True)
        ik = jnp.min(jnp.where(sel == best, lane_e, float(N_EXPERTS)), axis=1, keepdims=True)
        hit = lane_e == ik
        gk = jnp.sum(jnp.where(hit, scores, 0.0), axis=1, keepdims=True)
        sel = jnp.where(hit, -jnp.inf, sel)
        idx_acc = jnp.where(lane_o == k, ik, idx_acc)
        gate_acc = jnp.where(lane_o == k, gk, gate_acc)
        gsum = gsum + gk
    idx_ref[...] = idx_acc.astype(jnp.int32)
    gate_ref[...] = gate_acc / gsum * ROUTE_SCALE


def _outproj_call(ya, yb, yc, yd, w_out_bf, x, ada, prm, layer, rows_per_b, tm):
    m = x.shape[0]
    r = ada.shape[1]

    def full(arr):
        nd = arr.ndim
        return pl.BlockSpec(arr.shape, lambda i: (0,) * nd)

    grp = pl.BlockSpec((tm, W_GROUP), lambda i: (i, 0))
    row = pl.BlockSpec((tm, D_MODEL), lambda i: (i, 0))
    lane_out = pl.BlockSpec((tm, LANES), lambda i: (i, 0))

    def mod(colidx):
        return pl.BlockSpec((None, r, D_MODEL), _mod_spec(rows_per_b, tm, colidx))

    small = [prm["ln1_g"], prm["ln1_b"], prm["router_w"], prm["router_bias"]]
    return pl.pallas_call(
        _outproj_kernel,
        out_shape=(jax.ShapeDtypeStruct((m, D_MODEL), F32), jax.ShapeDtypeStruct((m, D_MODEL), F32),
                   jax.ShapeDtypeStruct((m, LANES), jnp.int32), jax.ShapeDtypeStruct((m, LANES), F32)),
        grid=(m // tm,),
        in_specs=[grp, grp, grp, grp,
                  pl.BlockSpec((None, D_MODEL, D_MODEL), lambda i: (layer, 0, 0)),
                  row, mod(2), mod(4), mod(3)] + [full(a) for a in small],
        out_specs=(row, row, lane_out, lane_out),
        compiler_params=_cparams(("arbitrary",)),
        name="outproj_ln1_router",
    )(ya, yb, yc, yd, w_out_bf, x, ada, ada, ada, *small)


def _moe_plan(idx_all):
    experts = jnp.arange(N_EXPERTS, dtype=jnp.int32)
    onehot = (idx_all[:, :, None] == experts[None, None, :]).astype(jnp.int32)
    per_tok = jnp.sum(onehot, axis=1)
    before = jnp.cumsum(per_tok, axis=0) - per_tok
    counts = jnp.sum(per_tok, axis=0)
    padded = (counts + MOE_BM - 1) // MOE_BM * MOE_BM
    pad_end = jnp.cumsum(padded)
    pad_start = pad_end - padded
    dest = jnp.sum(onehot * (before + pad_start[None, :])[:, None, :], axis=2).astype(jnp.int32).reshape(-1)
    blk_start = jnp.arange(MOE_NB, dtype=jnp.int32) * MOE_BM
    n_before = jnp.sum((pad_end[None, :] <= blk_start[:, None]).astype(jnp.int32), axis=1)
    blk_expert = jnp.minimum(n_before, N_EXPERTS - 1).astype(jnp.int32)
    blk_onehot = (blk_expert[:, None] == experts[None, :]).astype(jnp.int32)
    real_end = jnp.sum(blk_onehot * (pad_start + counts)[None, :], axis=1)
    blk_valid = jnp.clip(real_end - blk_start, 0, MOE_BM).astype(jnp.int32)
    n_used = (pad_end[-1] // MOE_BM).astype(jnp.int32).reshape(1)
    return dest, blk_expert, blk_valid, n_used


def _shared_ffn(h, wg, wu, wd):
    hb = h.astype(BF16)
    return _dot((_silu(_dot(hb, wg)) * _dot(hb, wu)).astype(BF16), wd)


def _dispatch_kernel(dest_ref, h2_ref, wg_ref, wu_ref, wd_ref, *refs, aliased):
    if aliased:
        _, xs_ref, sh_ref, sem = refs
    else:
        xs_ref, sh_ref, sem = refs
    tm = h2_ref.shape[0]
    base = pl.program_id(0) * tm

    def issue(t, carry):
        for k in range(TOP_K):
            d = dest_ref[(base + t) * TOP_K + k]
            pltpu.make_async_copy(h2_ref.at[pl.ds(t, 1)], xs_ref.at[pl.ds(d, 1)], sem).start()
        return carry

    lax.fori_loop(0, tm, issue, 0)
    sh_ref[...] = _shared_ffn(h2_ref[...], wg_ref[...], wu_ref[...], wd_ref[...])
    for k in range(TOP_K):
        pltpu.make_async_copy(h2_ref, xs_ref.at[pl.ds(0, tm)], sem).wait()


def _dispatch_call(dest, h2, sh_wg, sh_wu, sh_wd, layer, tm, xs_prev=None):
    m = h2.shape[0]
    aliased = xs_prev is not None
    wspec_in = pl.BlockSpec((None, D_MODEL, D_EXPERT), lambda i, d: (layer, 0, 0))
    wspec_out = pl.BlockSpec((None, D_EXPERT, D_MODEL), lambda i, d: (layer, 0, 0))
    in_specs = [pl.BlockSpec((tm, D_MODEL), lambda i, d: (i, 0)), wspec_in, wspec_in, wspec_out]
    args = [dest, h2, sh_wg, sh_wu, sh_wd]
    if aliased:
        in_specs.append(pl.BlockSpec(memory_space=pl.ANY))
        args.append(xs_prev)
    grid_spec = pltpu.PrefetchScalarGridSpec(
        num_scalar_prefetch=1,
        grid=(m // tm,),
        in_specs=in_specs,
        out_specs=(pl.BlockSpec(memory_space=pl.ANY), pl.BlockSpec((tm, D_MODEL), lambda i, d: (i, 0))),
        scratch_shapes=[pltpu.SemaphoreType.DMA],
    )
    return pl.pallas_call(
        functools.partial(_dispatch_kernel, aliased=aliased),
        out_shape=(jax.ShapeDtypeStruct((MOE_ROWS, D_MODEL), F32), jax.ShapeDtypeStruct((m, D_MODEL), F32)),
        grid_spec=grid_spec,
        input_output_aliases={5: 0} if aliased else {},
        compiler_params=_cparams(("arbitrary",), has_side_effects=True),
        name="moe_dispatch",
    )(*args)


def _expert_kernel(be_ref, nv_ref, nu_ref, x_ref, wg_ref, wu_ref, wd_ref, o_ref, wg_s, wu_s, wd_s):
    i = pl.program_id(0)

    @pl.when(i < nu_ref[0])
    def _():
        prev = be_ref[jnp.maximum(i - 1, 0)]

        @pl.when((i == 0) | (be_ref[i] != prev))
        def _():
            wg_s[...] = wg_ref[...].astype(BF16)
            wu_s[...] = wu_ref[...].astype(BF16)
            wd_s[...] = wd_ref[...].astype(BF16)

        rows = lax.broadcasted_iota(jnp.int32, (MOE_BM, 1), 0)
        x = jnp.where(rows < nv_ref[i], x_ref[...], 0.0).astype(BF16)
        g = _dot(x, wg_s[...])
        u = _dot(x, wu_s[...])
        o_ref[...] = _dot((_silu(g) * u).astype(BF16), wd_s[...])

    @pl.when(i >= nu_ref[0])
    def _():
        o_ref[...] = jnp.zeros(o_ref.shape, F32)


def _expert_call(blk_expert, blk_valid, n_used, xs, w_gate, w_up, w_down, layer):
    def blk(i, be, nv, nu):
        return jnp.minimum(i, nu[0] - 1)

    def wmap(i, be, nv, nu):
        return (layer, be[blk(i, be, nv, nu)], 0, 0)

    rows = pl.BlockSpec((MOE_BM, D_MODEL), lambda i, be, nv, nu: (blk(i, be, nv, nu), 0))
    grid_spec = pltpu.PrefetchScalarGridSpec(
        num_scalar_prefetch=3,
        grid=(MOE_NB,),
        in_specs=[rows,
                  pl.BlockSpec((None, None, D_MODEL, D_EXPERT), wmap),
                  pl.BlockSpec((None, None, D_MODEL, D_EXPERT), wmap),
                  pl.BlockSpec((None, None, D_EXPERT, D_MODEL), wmap)],
        out_specs=pl.BlockSpec((MOE_BM, D_MODEL), lambda i, be, nv, nu: (i, 0)),
        scratch_shapes=[pltpu.VMEM((D_MODEL, D_EXPERT), BF16), pltpu.VMEM((D_MODEL, D_EXPERT), BF16),
                        pltpu.VMEM((D_EXPERT, D_MODEL), BF16)],
    )
    return pl.pallas_call(
        _expert_kernel,
        out_shape=jax.ShapeDtypeStruct((MOE_ROWS, D_MODEL), F32),
        grid_spec=grid_spec,
        compiler_params=_cparams(("arbitrary",)),
        name="moe_experts",
    )(blk_expert, blk_valid, n_used, xs, w_gate, w_up, w_down)


def _combine_kernel(dest_ref, rows_ref, gate_ref, sh_ref, x1_ref, g2_ref, lg_ref, lb_ref, o_ref, gbuf, sem):
    tm = x1_ref.shape[0]
    base = pl.program_id(0) * tm

    def issue(t, carry):
        for k in range(TOP_K):
            d = dest_ref[(base + t) * TOP_K + k]
            pltpu.make_async_copy(rows_ref.at[pl.ds(d, 1)], gbuf.at[k, pl.ds(t, 1)], sem).start()
        return carry

    lax.fori_loop(0, tm, issue, 0)
    for k in range(TOP_K):
        pltpu.make_async_copy(rows_ref.at[pl.ds(0, tm)], gbuf.at[k], sem).wait()
    gate = gate_ref[...]
    moe = sh_ref[...]
    for k in range(TOP_K):
        moe = moe + gate[:, k:k + 1] * gbuf[k]
    o_ref[...] = _layer_norm(ALPHA * x1_ref[...] + g2_ref[...] * moe, lg_ref[...], lb_ref[...])


def _combine_call(dest, out_rows, gate, sh, x1, ada, prm, rows_per_b, tm):
    m = x1.shape[0]
    r = ada.shape[1]
    row = pl.BlockSpec((tm, D_MODEL), lambda i, d: (i, 0))
    vec = pl.BlockSpec((1, D_MODEL), lambda i, d: (0, 0))
    grid_spec = pltpu.PrefetchScalarGridSpec(
        num_scalar_prefetch=1,
        grid=(m // tm,),
        in_specs=[pl.BlockSpec(memory_space=pl.ANY),
                  pl.BlockSpec((tm, LANES), lambda i, d: (i, 0)), row, row,
                  pl.BlockSpec((None, r, D_MODEL), _mod_spec(rows_per_b, tm, 5)), vec, vec],
        out_specs=row,
        scratch_shapes=[pltpu.VMEM((TOP_K, tm, D_MODEL), F32), pltpu.SemaphoreType.DMA],
    )
    return pl.pallas_call(
        _combine_kernel,
        out_shape=jax.ShapeDtypeStruct((m, D_MODEL), F32),
        grid_spec=grid_spec,
        compiler_params=_cparams(("arbitrary",)),
        name="moe_combine",
    )(dest, out_rows, gate, sh, x1, ada, prm["ln2_g"], prm["ln2_b"])


def _block_diag(w):
    eye = jnp.eye(LRU_BLOCKS, dtype=w.dtype)
    full = jnp.einsum("hcd,hg->hcgd", w, eye).reshape(W_GROUP, W_GROUP)
    return full.astype(BF16)


def _rope_tables(pos):
    half = HEAD_DIM // 2
    freqs = ROPE_THETA ** (-jnp.arange(half, dtype=F32) / half)
    ang = pos.astype(F32)[:, None] * freqs[None, :]
    cos = jnp.cos(ang)
    sin = jnp.sin(ang)
    cos_t = jnp.concatenate([cos, cos, cos, cos], axis=1)
    sin_t = jnp.concatenate([-sin, sin, -sin, sin], axis=1)
    return cos_t, sin_t


def kernel(x_prompt, x_sample, cache_k, cache_v, state_conv_a, state_lru, state_conv_c, state_conv_d, page_table, c_prompt, c_sample, w_ada, b_ada, w_in, conv_a_w, conv_a_b, lru_wr, lru_br, lru_wi, lru_bi, lru_lambda, lam_q1, lam_k1, lam_q2, lam_k2, subln_g, conv_c_w, conv_c_b, ln_c_g, ln_c_b, conv_d_w, w_out, ln1_g, ln1_b, ln2_g, ln2_b, router_w, router_bias, exp_w_gate, exp_w_up, exp_w_down, sh_w_gate, sh_w_up, sh_w_down):
    n_pool = cache_k.shape[1]
    c_all = jnp.concatenate([c_prompt, c_sample, jnp.zeros((16 - BATCH - DEC_BATCH, D_MODEL), F32)], axis=0)
    ada_all = _ada_call(c_all, w_ada, b_ada)
    w_in_bf = w_in.astype(BF16)
    w_out_bf = w_out.astype(BF16)
    sh_wg_bf = sh_w_gate.astype(BF16)
    sh_wu_bf = sh_w_up.astype(BF16)
    sh_wd_bf = sh_w_down.astype(BF16)
    cache_k4 = cache_k.reshape(DEPTH, n_pool, PAGE_SIZE * N_HEADS, LANES)
    cache_v4 = cache_v.reshape(DEPTH, n_pool, PAGE_SIZE * N_HEADS, LANES)
    cos_p, sin_p = _rope_tables(jnp.arange(SEQ, dtype=jnp.int32))
    cos_s, sin_s = _rope_tables(jnp.full((1,), PAST_LEN, jnp.int32))
    lane = jnp.arange(LANES)
    comp_mask = jnp.stack([lane < HEAD_DIM, lane >= HEAD_DIM]).astype(F32)

    xp = x_prompt.reshape(T_PROMPT, D_MODEL)
    xs = x_sample.reshape(DEC_BATCH, D_MODEL)
    outs = [[] for _ in range(12)]
    for l in range(DEPTH):
        lam_init = 0.8 - 0.6 * math.exp(-0.3 * l)
        prm = dict(
            conv_a_w=conv_a_w[l], conv_a_b=conv_a_b[l][None], wr=_block_diag(lru_wr[l]), br=lru_br[l][None],
            wi=_block_diag(lru_wi[l]), bi=lru_bi[l][None], lam=lru_lambda[l][None],
            conv_c_w=conv_c_w[l], conv_c_b=conv_c_b[l][None], ln_c_g=ln_c_g[l][None], ln_c_b=ln_c_b[l][None],
            conv_d_w=conv_d_w[l],
            lam_q1=lam_q1[l][None], lam_k1=lam_k1[l][None], lam_q2=lam_q2[l][None], lam_k2=lam_k2[l][None],
            subln_g=subln_g[l][None], subln_g_col=subln_g[l][:, None], lam_init=jnp.full((1, 1), lam_init, F32),
            ln1_g=ln1_g[l][None], ln1_b=ln1_b[l][None], ln2_g=ln2_g[l][None], ln2_b=ln2_b[l][None],
            router_w=router_w[l], router_bias=router_bias[l][None],
        )
        ada_p = ada_all[l, :BATCH].reshape(BATCH, 1, 6 * D_MODEL)
        ada_s = ada_all[l, BATCH:BATCH + DEC_BATCH].reshape(1, DEC_BATCH, 6 * D_MODEL)

        proj_p = _inproj_call(xp, ada_p, w_in_bf, l, SEQ, 1024, 512).reshape(BATCH, SEQ, D_IN)
        ya, yc, yd, q_p, k_p, sa, sh, sc, sd = _mixer_prompt_call(proj_p, cos_p, sin_p, prm)
        yb = _flash_call(q_p, k_p, proj_p, prm)
        outs[0].append(k_p)
        outs[1].append(proj_p[:, :, 4 * W_GROUP:5 * W_GROUP])
        outs[2].append(sa)
        outs[3].append(sh.reshape(BATCH, W_GROUP))
        outs[4].append(sc)
        outs[5].append(sd)

        proj_s = _inproj_call(xs, ada_s, w_in_bf, l, DEC_BATCH, DEC_BATCH, 512)
        ya_s, yc_s, yd_s, q_s, k_s, sa_s, sh_s, sc_s, sd_s = _mixer_sample_call(
            proj_s, cos_s, sin_s, prm,
            state_conv_a[l].transpose(1, 0, 2), state_lru[l], state_conv_c[l].transpose(1, 0, 2),
            state_conv_d[l].transpose(1, 0, 2))
        v_s = proj_s[:, 4 * W_GROUP:5 * W_GROUP]
        q4 = q_s.reshape(DEC_BATCH, 1, N_HEADS, LANES) * (HEAD_DIM ** -0.5)
        q8 = (q4 * comp_mask[None, :, None, :]).reshape(DEC_BATCH, 2 * N_HEADS, LANES).astype(BF16)
        pad4 = jnp.zeros((DEC_BATCH, N_HEADS, LANES), F32)
        k_new8 = jnp.concatenate([k_s.reshape(DEC_BATCH, N_HEADS, LANES), pad4], axis=1)
        v_new8 = jnp.concatenate([v_s.reshape(DEC_BATCH, N_HEADS, LANES), pad4], axis=1)
        yb_s = _decode_call(q8, k_new8, v_new8, cache_k4, cache_v4, page_table, prm, l)
        yb_s = yb_s.reshape(DEC_BATCH, W_GROUP).astype(BF16)
        outs[6].append(k_s.reshape(DEC_BATCH, 1, N_HEADS, 2 * HEAD_DIM))
        outs[7].append(v_s.reshape(DEC_BATCH, 1, N_HEADS, 2 * HEAD_DIM))
        outs[8].append(sa_s.transpose(1, 0, 2))
        outs[9].append(sh_s)
        outs[10].append(sc_s.transpose(1, 0, 2))
        outs[11].append(sd_s.transpose(1, 0, 2))

        flat = lambda t: t.reshape(T_PROMPT, W_GROUP)
        x1_p, h2_p, idx_p, gate_p = _outproj_call(flat(ya), flat(yb), flat(yc), flat(yd), w_out_bf, xp, ada_p, prm, l, SEQ, 256)
        x1_s, h2_s, idx_s, gate_s = _outproj_call(ya_s, yb_s, yc_s, yd_s, w_out_bf, xs, ada_s, prm, l, DEC_BATCH, DEC_BATCH)

        idx_all = jnp.concatenate([idx_p[:, :TOP_K], idx_s[:, :TOP_K]], axis=0)
        dest, blk_expert, blk_valid, n_used = _moe_plan(idx_all)
        dest_p, dest_s = dest[:T_PROMPT * TOP_K], dest[T_PROMPT * TOP_K:]
        x_sorted, shared_p = _dispatch_call(dest_p, h2_p, sh_wg_bf, sh_wu_bf, sh_wd_bf, l, 256)
        x_sorted, shared_s = _dispatch_call(dest_s, h2_s, sh_wg_bf, sh_wu_bf, sh_wd_bf, l, DEC_BATCH, xs_prev=x_sorted)
        out_rows = _expert_call(blk_expert, blk_valid, n_used, x_sorted, exp_w_gate, exp_w_up, exp_w_down, l)
        xp = _combine_call(dest_p, out_rows, gate_p, shared_p, x1_p, ada_p, prm, SEQ, 128)
        xs = _combine_call(dest_s, out_rows, gate_s, shared_s, x1_s, ada_s, prm, DEC_BATCH, DEC_BATCH)

    st = [jnp.stack(o, axis=0) for o in outs]
    kv_shape = (DEPTH, BATCH, SEQ // PAGE_SIZE, PAGE_SIZE, N_HEADS, 2 * HEAD_DIM)
    return (xp.reshape(BATCH, SEQ, D_MODEL), xs.reshape(DEC_BATCH, 1, D_MODEL),
            st[0].reshape(kv_shape), st[1].reshape(kv_shape), st[2], st[3], st[4], st[5],
            st[6], st[7], st[8], st[9], st[10], st[11])
```

```python
import functools
import math

import jax
import jax.numpy as jnp
from jax import lax
from jax.experimental import pallas as pl
from jax.experimental.pallas import tpu as pltpu

F32 = jnp.float32
BF16 = jnp.bfloat16

D_MODEL = 2048
BATCH = 2
SEQ = 4096
DEPTH = 4
DEC_BATCH = 8
PAST_LEN = 16384
PAGE_SIZE = 128
W_GROUP = 512
N_IN_PARTS = 10
D_IN = N_IN_PARTS * W_GROUP
CONV_A = 4
LRU_BLOCKS = 8
LRU_BLOCK_DIM = W_GROUP // LRU_BLOCKS
LRU_C = 8.0
HEAD_DIM = 64
N_HEADS = 4
ROPE_THETA = 10000.0
CONV_C = 31
CONV_D = 3
N_EXPERTS = 64
TOP_K = 8
D_EXPERT = 512
ROUTE_SCALE = 2.5
LN_EPS = 1e-5
ALPHA = (2.0 * DEPTH) ** 0.25
NEG_INF = -1e30

LANES = 128
VMEM_LIMIT = 56 * 1024 * 1024

T_PROMPT = BATCH * SEQ
T_ALL = T_PROMPT + DEC_BATCH
MOE_BM = 256
MOE_NB = -(-T_ALL * TOP_K // MOE_BM) + N_EXPERTS
N_PAGES = PAST_LEN // PAGE_SIZE
DEC_G = 8


def _cparams(sem, **kw):
    return pltpu.CompilerParams(dimension_semantics=sem, vmem_limit_bytes=VMEM_LIMIT, **kw)


def _dot(a, b):
    return jnp.dot(a, b, preferred_element_type=F32)


def _dot_nt(a, b):
    return lax.dot_general(a, b, (((1,), (1,)), ((), ())), preferred_element_type=F32)


def _sigmoid(x):
    return 1.0 / (1.0 + jnp.exp(-x))


def _silu(x):
    return x * _sigmoid(x)


def _layer_norm(z, g, b):
    mu = jnp.mean(z, axis=-1, keepdims=True)
    zc = z - mu
    var = jnp.mean(zc * zc, axis=-1, keepdims=True)
    return zc * lax.rsqrt(var + LN_EPS) * g + b


def _ada_kernel(c_ref, w_ref, b_ref, o_ref):
    c = c_ref[...]
    o_ref[...] = _dot(_silu(c).astype(BF16), w_ref[...].astype(BF16)) + b_ref[...]


def _ada_call(c_all, w_ada, b_ada):
    r = c_all.shape[0]
    tn = 1024
    return pl.pallas_call(
        _ada_kernel,
        out_shape=jax.ShapeDtypeStruct((DEPTH, r, 6 * D_MODEL), F32),
        grid=(DEPTH, 6 * D_MODEL // tn),
        in_specs=[
            pl.BlockSpec((r, D_MODEL), lambda l, n: (0, 0)),
            pl.BlockSpec((None, D_MODEL, tn), lambda l, n: (l, 0, n)),
            pl.BlockSpec((None, 1, tn), lambda l, n: (l, 0, n)),
        ],
        out_specs=pl.BlockSpec((None, r, tn), lambda l, n: (l, 0, n)),
        compiler_params=_cparams(("arbitrary", "arbitrary")),
        name="ada",
    )(c_all, w_ada, b_ada.reshape(DEPTH, 1, 6 * D_MODEL))


def _inproj_kernel(x_ref, sc_ref, sh_ref, w_ref, o_ref, h_scr):
    @pl.when(pl.program_id(1) == 0)
    def _():
        h_scr[...] = (x_ref[...] * (1.0 + sc_ref[...]) + sh_ref[...]).astype(BF16)

    o_ref[...] = _dot(h_scr[...], w_ref[...])


def _mod_spec(rows_per_b, tm, col):
    def imap(m, *_):
        return ((m * tm) // rows_per_b, 0, col)
    return imap


def _inproj_call(x, ada, w_in_bf, layer, rows_per_b, tm, tn):
    m = x.shape[0]
    r = ada.shape[1]
    return pl.pallas_call(
        _inproj_kernel,
        out_shape=jax.ShapeDtypeStruct((m, D_IN), F32),
        grid=(m // tm, D_IN // tn),
        in_specs=[
            pl.BlockSpec((tm, D_MODEL), lambda i, n: (i, 0)),
            pl.BlockSpec((None, r, D_MODEL), _mod_spec(rows_per_b, tm, 1)),
            pl.BlockSpec((None, r, D_MODEL), _mod_spec(rows_per_b, tm, 0)),
            pl.BlockSpec((None, D_MODEL, tn), lambda i, n: (layer, 0, n)),
        ],
        out_specs=pl.BlockSpec((tm, tn), lambda i, n: (i, n)),
        scratch_shapes=[pltpu.VMEM((tm, D_MODEL), BF16)],
        compiler_params=_cparams(("arbitrary", "arbitrary")),
        name="inproj",
    )(x, ada, ada, w_in_bf)


def _softplus(z):
    return jnp.maximum(z, 0.0) + jnp.log1p(jnp.exp(-jnp.abs(z)))


def _gelu_tanh(x):
    return 0.5 * x * (1.0 + jnp.tanh(math.sqrt(2.0 / math.pi) * (x + 0.044715 * (x * x * x))))


def _lru_gates(xa, wr, br, wi, bi, lam):
    xb = xa.astype(BF16)
    r = _sigmoid(_dot(xb, wr) + br)
    i = _sigmoid(_dot(xb, wi) + bi)
    log_a = -LRU_C * r * _softplus(-lam)
    a = jnp.exp(log_a)
    u = jnp.sqrt(jnp.tanh(-log_a) * (1.0 + a * a)) * (i * xa)
    return a, u


def _rope(x, cos_t, sin_t):
    lane = lax.broadcasted_iota(jnp.int32, x.shape, 1)
    first_half = (lane & (HEAD_DIM // 2)) == 0
    w = x.shape[1]
    partner = jnp.where(first_half, pltpu.roll(x, w - HEAD_DIM // 2, axis=1), pltpu.roll(x, HEAD_DIM // 2, axis=1))
    return x * cos_t + partner * sin_t


def _tile4(t):
    return jnp.concatenate([t, t, t, t], axis=1)


def _conv_from_buf(buf_ref, w_ref, hist, n_taps, rows, rb):
    outs = []
    for r0 in range(0, rows, rb):
        acc = None
        for j in range(n_taps):
            term = buf_ref[pl.ds(hist - (n_taps - 1) + j + r0, rb), :] * w_ref[pl.ds(j, 1), :]
            acc = term if acc is None else acc + term
        outs.append(acc)
    return jnp.concatenate(outs, axis=0) if len(outs) > 1 else outs[0]


def _linear_scan(a, b):
    n = a.shape[0]
    rows = lax.broadcasted_iota(jnp.int32, a.shape, 0)
    d = 1
    while d < n:
        keep = rows >= d
        a_sh = jnp.where(keep, pltpu.roll(a, d, axis=0), 1.0)
        b_sh = jnp.where(keep, pltpu.roll(b, d, axis=0), 0.0)
        b = a * b_sh + b
        a = a * a_sh
        d *= 2
    return a, b


MIX_TC = 256
HIST_A = 8
HIST_C = 32
HIST_D = 8
CONV_RB = 32


def _mixer_prompt_kernel(ax_ref, ag_ref, bq_ref, bk_ref, cv_ref, cg_ref, db_ref, dc_ref, dx_ref,
                         cos_ref, sin_ref, caw_ref, cab_ref, wr_ref, br_ref, wi_ref, bi_ref, lam_ref,
                         ccw_ref, ccb_ref, lng_ref, lnb_ref, cdw_ref,
                         ya_ref, yc_ref, yd_ref, q_ref, k_ref, sa_ref, sh_ref, sc_ref, sd_ref,
                         xa_buf, uc_buf, dd_buf, h_buf):
    tc = MIX_TC

    @pl.when(pl.program_id(1) == 0)
    def _():
        xa_buf[pl.ds(0, HIST_A), :] = jnp.zeros((HIST_A, W_GROUP), F32)
        uc_buf[pl.ds(0, HIST_C), :] = jnp.zeros((HIST_C, W_GROUP), F32)
        dd_buf[pl.ds(0, HIST_D), :] = jnp.zeros((HIST_D, W_GROUP), F32)
        h_buf[...] = jnp.zeros((1, W_GROUP), F32)

    xa_buf[pl.ds(HIST_A, tc), :] = ax_ref[...]
    xa = _conv_from_buf(xa_buf, caw_ref, HIST_A, CONV_A, tc, CONV_RB) + cab_ref[...]
    a, u = _lru_gates(xa, wr_ref[...], br_ref[...], wi_ref[...], bi_ref[...], lam_ref[...])
    a_cum, b_cum = _linear_scan(a, u)
    h = a_cum * h_buf[...] + b_cum
    ya_ref[...] = (_gelu_tanh(ag_ref[...]) * h).astype(BF16)
    h_last = h[tc - 1:tc, :]
    h_buf[...] = h_last
    sh_ref[...] = h_last
    sa_ref[...] = xa_buf[pl.ds(HIST_A + tc - (CONV_A - 1), CONV_A - 1), :]
    xa_buf[pl.ds(0, HIST_A), :] = xa_buf[pl.ds(tc, HIST_A), :]

    uc_buf[pl.ds(HIST_C, tc), :] = cv_ref[...] * _sigmoid(cg_ref[...])
    yc = _conv_from_buf(uc_buf, ccw_ref, HIST_C, CONV_C, tc, CONV_RB) + ccb_ref[...]
    yc_ref[...] = _silu(_layer_norm(yc, lng_ref[...], lnb_ref[...])).astype(BF16)
    sc_ref[...] = uc_buf[pl.ds(HIST_C + tc - (CONV_C - 1), CONV_C - 1), :]
    uc_buf[pl.ds(0, HIST_C), :] = uc_buf[pl.ds(tc, HIST_C), :]

    dd_buf[pl.ds(HIST_D, tc), :] = dc_ref[...] * dx_ref[...]
    yd = _conv_from_buf(dd_buf, cdw_ref, HIST_D, CONV_D, tc, CONV_RB)
    yd_ref[...] = (db_ref[...] * yd).astype(BF16)
    sd_ref[...] = dd_buf[pl.ds(HIST_D + tc - (CONV_D - 1), CONV_D - 1), :]
    dd_buf[pl.ds(0, HIST_D), :] = dd_buf[pl.ds(tc, HIST_D), :]

    cos_t = _tile4(cos_ref[...])
    sin_t = _tile4(sin_ref[...])
    q_ref[...] = (_rope(bq_ref[...], cos_t, sin_t) * (HEAD_DIM ** -0.5)).astype(BF16)
    k_ref[...] = _rope(bk_ref[...], cos_t, sin_t)


def _mixer_prompt_call(proj, cos_t, sin_t, prm):
    tc = MIX_TC

    def col(j):
        return pl.BlockSpec((None, tc, W_GROUP), lambda b, c: (b, c, j))

    def full(arr):
        nd = arr.ndim
        return pl.BlockSpec(arr.shape, lambda b, c: (0,) * nd)

    small = [prm["conv_a_w"], prm["conv_a_b"], prm["wr"], prm["br"], prm["wi"], prm["bi"], prm["lam"],
             prm["conv_c_w"], prm["conv_c_b"], prm["ln_c_g"], prm["ln_c_b"], prm["conv_d_w"]]
    tok = pl.BlockSpec((None, tc, W_GROUP), lambda b, c: (b, c, 0))

    def state(n):
        return pl.BlockSpec((None, n, W_GROUP), lambda b, c: (b, 0, 0))

    out_shape = (
        jax.ShapeDtypeStruct((BATCH, SEQ, W_GROUP), BF16),
        jax.ShapeDtypeStruct((BATCH, SEQ, W_GROUP), BF16),
        jax.ShapeDtypeStruct((BATCH, SEQ, W_GROUP), BF16),
        jax.ShapeDtypeStruct((BATCH, SEQ, W_GROUP), BF16),
        jax.ShapeDtypeStruct((BATCH, SEQ, W_GROUP), F32),
        jax.ShapeDtypeStruct((BATCH, CONV_A - 1, W_GROUP), F32),
        jax.ShapeDtypeStruct((BATCH, 1, W_GROUP), F32),
        jax.ShapeDtypeStruct((BATCH, CONV_C - 1, W_GROUP), F32),
        jax.ShapeDtypeStruct((BATCH, CONV_D - 1, W_GROUP), F32),
    )
    return pl.pallas_call(
        _mixer_prompt_kernel,
        out_shape=out_shape,
        grid=(BATCH, SEQ // tc),
        in_specs=[col(0), col(1), col(2), col(3), col(5), col(6), col(7), col(8), col(9),
                  pl.BlockSpec((tc, LANES), lambda b, c: (c, 0)),
                  pl.BlockSpec((tc, LANES), lambda b, c: (c, 0))] + [full(a) for a in small],
        out_specs=(tok, tok, tok, tok, tok, state(CONV_A - 1), state(1), state(CONV_C - 1), state(CONV_D - 1)),
        scratch_shapes=[pltpu.VMEM((HIST_A + tc, W_GROUP), F32), pltpu.VMEM((HIST_C + tc, W_GROUP), F32),
                        pltpu.VMEM((HIST_D + tc, W_GROUP), F32), pltpu.VMEM((1, W_GROUP), F32)],
        compiler_params=_cparams(("arbitrary", "arbitrary")),
        name="mixer_prompt",
    )(*([proj] * 9), cos_t, sin_t, *small)


def _mixer_sample_kernel(ax_ref, ag_ref, bq_ref, bk_ref, cv_ref, cg_ref, db_ref, dc_ref, dx_ref,
                         cos_ref, sin_ref, caw_ref, cab_ref, wr_ref, br_ref, wi_ref, bi_ref, lam_ref,
                         ccw_ref, ccb_ref, lng_ref, lnb_ref, cdw_ref,
                         ba_ref, h0_ref, bc_ref, bd_ref,
                         ya_ref, yc_ref, yd_ref, q_ref, k_ref, sa_ref, sh_ref, sc_ref, sd_ref):
    ax = ax_ref[...]
    xa = cab_ref[...] + caw_ref[pl.ds(CONV_A - 1, 1), :] * ax
    for j in range(CONV_A - 1):
        xa = xa + caw_ref[pl.ds(j, 1), :] * ba_ref[j]
    a, u = _lru_gates(xa, wr_ref[...], br_ref[...], wi_ref[...], bi_ref[...], lam_ref[...])
    h = a * h0_ref[...] + u
    ya_ref[...] = (_gelu_tanh(ag_ref[...]) * h).astype(BF16)
    sh_ref[...] = h
    for j in range(CONV_A - 2):
        sa_ref[j] = ba_ref[j + 1]
    sa_ref[CONV_A - 2] = ax

    uc = cv_ref[...] * _sigmoid(cg_ref[...])
    yc = ccb_ref[...] + ccw_ref[pl.ds(CONV_C - 1, 1), :] * uc
    for j in range(CONV_C - 1):
        yc = yc + ccw_ref[pl.ds(j, 1), :] * bc_ref[j]
    yc_ref[...] = _silu(_layer_norm(yc, lng_ref[...], lnb_ref[...])).astype(BF16)
    for j in range(CONV_C - 2):
        sc_ref[j] = bc_ref[j + 1]
    sc_ref[CONV_C - 2] = uc

    dd = dc_ref[...] * dx_ref[...]
    yd = cdw_ref[pl.ds(CONV_D - 1, 1), :] * dd
    for j in range(CONV_D - 1):
        yd = yd + cdw_ref[pl.ds(j, 1), :] * bd_ref[j]
    yd_ref[...] = (db_ref[...] * yd).astype(BF16)
    for j in range(CONV_D - 2):
        sd_ref[j] = bd_ref[j + 1]
    sd_ref[CONV_D - 2] = dd

    cos_t = _tile4(cos_ref[...])
    sin_t = _tile4(sin_ref[...])
    q_ref[...] = _rope(bq_ref[...], cos_t, sin_t)
    k_ref[...] = _rope(bk_ref[...], cos_t, sin_t)


def _mixer_sample_call(proj, cos_t, sin_t, prm, buf_a, h0, buf_c, buf_d):
    nb = DEC_BATCH

    def col(j):
        return pl.BlockSpec((nb, W_GROUP), lambda i: (0, j))

    def full(arr):
        nd = arr.ndim
        return pl.BlockSpec(arr.shape, lambda i: (0,) * nd)

    small = [prm["conv_a_w"], prm["conv_a_b"], prm["wr"], prm["br"], prm["wi"], prm["bi"], prm["lam"],
             prm["conv_c_w"], prm["conv_c_b"], prm["ln_c_g"], prm["ln_c_b"], prm["conv_d_w"]]
    states = [buf_a, h0, buf_c, buf_d]
    tok_bf = jax.ShapeDtypeStruct((nb, W_GROUP), BF16)
    tok_f = jax.ShapeDtypeStruct((nb, W_GROUP), F32)
    out_shape = (tok_bf, tok_bf, tok_bf, tok_f, tok_f,
                 jax.ShapeDtypeStruct(buf_a.shape, F32), tok_f,
                 jax.ShapeDtypeStruct(buf_c.shape, F32), jax.ShapeDtypeStruct(buf_d.shape, F32))
    return pl.pallas_call(
        _mixer_sample_kernel,
        out_shape=out_shape,
        grid=(1,),
        in_specs=[col(0), col(1), col(2), col(3), col(5), col(6), col(7), col(8), col(9),
                  full(cos_t), full(sin_t)] + [full(a) for a in small] + [full(a) for a in states],
        out_specs=tuple(pl.BlockSpec(s.shape, lambda i, nd=len(s.shape): (0,) * nd) for s in out_shape),
        compiler_params=_cparams(("arbitrary",)),
        name="mixer_sample",
    )(*([proj] * 9), cos_t, sin_t, *small, *states)


def _diff_lambda(lq1_ref, lk1_ref, lq2_ref, lk2_ref, lam_init):
    s1 = jnp.sum(lq1_ref[...] * lk1_ref[...], axis=-1, keepdims=True)
    s2 = jnp.sum(lq2_ref[...] * lk2_ref[...], axis=-1, keepdims=True)
    return jnp.exp(s1) - jnp.exp(s2) + lam_init


def _diff_finish(o1, o2, lam, g, lam_init):
    od = o1 - lam * o2
    ms = jnp.mean(od * od, axis=-1, keepdims=True)
    return od * lax.rsqrt(ms + LN_EPS) * g * (1.0 - lam_init)


FLASH_T = 512


def _flash_kernel(q_ref, k_ref, v_ref, lq1_ref, lk1_ref, lq2_ref, lk2_ref, gcol_ref, li_ref, o_ref,
                  kb_scr, vt_scr, m_scr, l_scr, acc_scr):
    t = FLASH_T
    i = pl.program_id(2)

    @pl.when(i == 0)
    def _():
        for c in range(SEQ // t):
            kb_scr[c] = k_ref[pl.ds(c * t, t), :].astype(BF16)
            vt_scr[c] = v_ref[pl.ds(c * t, t), :].T.astype(BF16)

    q = q_ref[...]
    lane = lax.broadcasted_iota(jnp.int32, q.shape, 1)
    zero = jnp.zeros_like(q)
    qq = jnp.concatenate([jnp.where(lane < HEAD_DIM, q, zero), jnp.where(lane >= HEAD_DIM, q, zero)], axis=0)
    m_scr[...] = jnp.full(m_scr.shape, NEG_INF, F32)
    l_scr[...] = jnp.zeros(l_scr.shape, F32)
    acc_scr[...] = jnp.zeros(acc_scr.shape, F32)

    def step(j, masked):
        st = _dot_nt(kb_scr[j], qq)
        if masked:
            r = lax.broadcasted_iota(jnp.int32, st.shape, 0)
            c = lax.broadcasted_iota(jnp.int32, st.shape, 1)
            st = jnp.where(r <= jnp.where(c >= t, c - t, c), st, NEG_INF)
        m_prev = m_scr[...]
        m_new = jnp.maximum(m_prev, jnp.max(st, axis=0, keepdims=True))
        alpha = jnp.exp(m_prev - m_new)
        p = jnp.exp(st - m_new)
        l_scr[...] = alpha * l_scr[...] + jnp.sum(p, axis=0, keepdims=True)
        acc_scr[...] = alpha * acc_scr[...] + _dot(vt_scr[j], p.astype(BF16))
        m_scr[...] = m_new

    def body(j, carry):
        step(j, False)
        return carry

    lax.fori_loop(0, i, body, 0)
    step(i, True)

    o = acc_scr[...] / l_scr[...]
    lam_init = li_ref[...]
    lam = _diff_lambda(lq1_ref, lk1_ref, lq2_ref, lk2_ref, lam_init)
    od = o[:, :t] - lam * o[:, t:]
    ms = jnp.mean(od * od, axis=0, keepdims=True)
    y = od * lax.rsqrt(ms + LN_EPS) * gcol_ref[...] * (1.0 - lam_init)
    o_ref[...] = y.T.astype(BF16)


def _flash_call(q, k, proj, prm):
    t = FLASH_T
    v_col0 = 4 * W_GROUP // LANES

    def full(arr):
        nd = arr.ndim
        return pl.BlockSpec(arr.shape, lambda b, h, i: (0,) * nd)

    small = [prm["lam_q1"], prm["lam_k1"], prm["lam_q2"], prm["lam_k2"], prm["subln_g_col"], prm["lam_init"]]
    return pl.pallas_call(
        _flash_kernel,
        out_shape=jax.ShapeDtypeStruct((BATCH, SEQ, W_GROUP), BF16),
        grid=(BATCH, N_HEADS, SEQ // t),
        in_specs=[pl.BlockSpec((None, t, LANES), lambda b, h, i: (b, i, h)),
                  pl.BlockSpec((None, SEQ, LANES), lambda b, h, i: (b, 0, h)),
                  pl.BlockSpec((None, SEQ, LANES), lambda b, h, i: (b, 0, v_col0 + h))] + [full(a) for a in small],
        out_specs=pl.BlockSpec((None, t, LANES), lambda b, h, i: (b, i, h)),
        scratch_shapes=[pltpu.VMEM((SEQ // t, t, LANES), BF16), pltpu.VMEM((SEQ // t, LANES, t), BF16),
                        pltpu.VMEM((1, 2 * t), F32), pltpu.VMEM((1, 2 * t), F32), pltpu.VMEM((LANES, 2 * t), F32)],
        compiler_params=_cparams(("arbitrary", "arbitrary", "arbitrary")),
        name="flash",
    )(q, k, proj, *small)


def _softmax_update(s, v, m_scr, l_scr, acc_scr):
    m_prev = m_scr[...]
    m_new = jnp.maximum(m_prev, jnp.max(s, axis=1, keepdims=True))
    alpha = jnp.exp(m_prev - m_new)
    p = jnp.exp(s - m_new)
    l_scr[...] = alpha * l_scr[...] + jnp.sum(p, axis=1, keepdims=True)
    acc_scr[...] = alpha * acc_scr[...] + _dot(p.astype(BF16), v)
    m_scr[...] = m_new


def _decode_kernel(pt_ref, q_ref, *refs):
    k_refs = refs[:DEC_G]
    v_refs = refs[DEC_G:2 * DEC_G]
    (kn_ref, vn_ref, lq1_ref, lk1_ref, lq2_ref, lk2_ref, g_ref, li_ref, o_ref, m_scr, l_scr, acc_scr) = refs[2 * DEC_G:]
    s_idx = pl.program_id(1)

    @pl.when(s_idx == 0)
    def _():
        m_scr[...] = jnp.full(m_scr.shape, NEG_INF, F32)
        l_scr[...] = jnp.zeros(l_scr.shape, F32)
        acc_scr[...] = jnp.zeros(acc_scr.shape, F32)

    q = q_ref[...]
    n_kv = PAGE_SIZE * N_HEADS
    row = lax.broadcasted_iota(jnp.int32, (2 * N_HEADS, n_kv), 0)
    col = lax.broadcasted_iota(jnp.int32, (2 * N_HEADS, n_kv), 1)
    same_head = (col & (N_HEADS - 1)) == (row & (N_HEADS - 1))
    s = jnp.concatenate([jnp.where(same_head, _dot_nt(q, k_refs[g][...].astype(BF16)), NEG_INF)
                         for g in range(DEC_G)], axis=1)
    m_prev = m_scr[...]
    m_new = jnp.maximum(m_prev, jnp.max(s, axis=1, keepdims=True))
    alpha = jnp.exp(m_prev - m_new)
    p = jnp.exp(s - m_new)
    l_scr[...] = alpha * l_scr[...] + jnp.sum(p, axis=1, keepdims=True)
    pb = p.astype(BF16)
    pv = _dot(pb[:, :n_kv], v_refs[0][...].astype(BF16))
    for g in range(1, DEC_G):
        pv = pv + _dot(pb[:, g * n_kv:(g + 1) * n_kv], v_refs[g][...].astype(BF16))
    acc_scr[...] = alpha * acc_scr[...] + pv
    m_scr[...] = m_new

    @pl.when(s_idx == pl.num_programs(1) - 1)
    def _():
        kn = kn_ref[...].astype(BF16)
        vn = vn_ref[...].astype(BF16)
        r8 = lax.broadcasted_iota(jnp.int32, (2 * N_HEADS, 2 * N_HEADS), 0)
        c8 = lax.broadcasted_iota(jnp.int32, (2 * N_HEADS, 2 * N_HEADS), 1)
        s = jnp.where(c8 == (r8 & (N_HEADS - 1)), _dot_nt(q, kn), NEG_INF)
        _softmax_update(s, vn, m_scr, l_scr, acc_scr)
        o = acc_scr[...] / l_scr[...]
        lam_init = li_ref[...]
        lam = _diff_lambda(lq1_ref, lk1_ref, lq2_ref, lk2_ref, lam_init)
        o_ref[...] = _diff_finish(o[:N_HEADS], o[N_HEADS:], lam, g_ref[...], lam_init)


def _decode_call(q8, k_new8, v_new8, cache_k4, cache_v4, page_table, prm, layer):
    n_kv = PAGE_SIZE * N_HEADS
    steps = N_PAGES // DEC_G

    def page_spec(g):
        return pl.BlockSpec((None, None, n_kv, LANES), lambda b, s, pt: (layer, pt[b, s * DEC_G + g], 0, 0))

    def full(arr):
        nd = arr.ndim
        return pl.BlockSpec(arr.shape, lambda b, s, pt: (0,) * nd)

    row8 = pl.BlockSpec((None, 2 * N_HEADS, LANES), lambda b, s, pt: (b, 0, 0))
    small = [prm["lam_q1"], prm["lam_k1"], prm["lam_q2"], prm["lam_k2"], prm["subln_g"], prm["lam_init"]]
    grid_spec = pltpu.PrefetchScalarGridSpec(
        num_scalar_prefetch=1,
        grid=(DEC_BATCH, steps),
        in_specs=[row8] + [page_spec(g) for g in range(DEC_G)] + [page_spec(g) for g in range(DEC_G)]
                 + [row8, row8] + [full(a) for a in small],
        out_specs=pl.BlockSpec((None, N_HEADS, LANES), lambda b, s, pt: (b, 0, 0)),
        scratch_shapes=[pltpu.VMEM((2 * N_HEADS, 1), F32), pltpu.VMEM((2 * N_HEADS, 1), F32),
                        pltpu.VMEM((2 * N_HEADS, LANES), F32)],
    )
    return pl.pallas_call(
        _decode_kernel,
        out_shape=jax.ShapeDtypeStruct((DEC_BATCH, N_HEADS, LANES), F32),
        grid_spec=grid_spec,
        compiler_params=_cparams(("arbitrary", "arbitrary")),
        name="decode_attn",
    )(page_table, q8, *([cache_k4] * DEC_G), *([cache_v4] * DEC_G), k_new8, v_new8, *small)


MOE_CAP = -(-T_ALL // MOE_BM) * MOE_BM
MOE_CAP_BLKS = MOE_CAP // MOE_BM
MOE_SLOTS = N_EXPERTS * MOE_CAP
HALF = D_MODEL // 2
U32 = jnp.uint32


def _pack_rows(x):
    a = lax.bitcast_convert_type(x[:, :HALF].astype(BF16).astype(F32), U32)
    b = lax.bitcast_convert_type(x[:, HALF:].astype(BF16).astype(F32), U32)
    return a | (b >> 16)


def _unpack_rows(w):
    a = lax.bitcast_convert_type(w & U32(0xFFFF0000), F32)
    b = lax.bitcast_convert_type(w << 16, F32)
    return a, b


def _shared_ffn(h, wg, wu, wd):
    hb = h.astype(BF16)
    return _dot((_silu(_dot(hb, wg)) * _dot(hb, wu)).astype(BF16), wd)


def _outproj_kernel(ya_ref, yb_ref, yc_ref, yd_ref, w_ref, x_ref, g1_ref, sc2_ref, sh2_ref, lg_ref, lb_ref,
                    rw_ref, rb_ref, swg_ref, swu_ref, swd_ref, cnt_in_ref, *refs, aliased):
    if aliased:
        refs = refs[1:]
    x1_ref, sh_ref, gate_ref, dest_ref, cnt_ref, xs_ref, xpk, dest_v, dest_s, cnt_scr, sem_rows, sem_d = refs
    i = pl.program_id(0)
    last = pl.num_programs(0) - 1
    tm = x_ref.shape[0]

    @pl.when(i == 0)
    def _():
        cnt_scr[...] = cnt_in_ref[...]

    mix = (_dot(ya_ref[...], w_ref[pl.ds(0, W_GROUP), :]) + _dot(yb_ref[...], w_ref[pl.ds(W_GROUP, W_GROUP), :])
           + _dot(yc_ref[...], w_ref[pl.ds(2 * W_GROUP, W_GROUP), :]) + _dot(yd_ref[...], w_ref[pl.ds(3 * W_GROUP, W_GROUP), :]))
    x1 = _layer_norm(ALPHA * x_ref[...] + g1_ref[...] * mix, lg_ref[...], lb_ref[...])
    x1_ref[...] = x1
    h2 = x1 * (1.0 + sc2_ref[...]) + sh2_ref[...]

    scores = _sigmoid(_dot(h2.astype(BF16), rw_ref[...]))
    sel = scores + rb_ref[...]
    lane_e = lax.broadcasted_iota(jnp.int32, (tm, N_EXPERTS), 1).astype(F32)
    lane_o = lax.broadcasted_iota(jnp.int32, (tm, LANES), 1)
    gate_acc = jnp.zeros((tm, LANES), F32)
    gsum = jnp.zeros((tm, 1), F32)
    chosen = jnp.zeros((tm, N_EXPERTS), F32)
    picks = []
    for k in range(TOP_K):
        best = jnp.max(sel, axis=1, keepdims=True)
        ik = jnp.min(jnp.where(sel == best, lane_e, float(N_EXPERTS)), axis=1, keepdims=True)
        hit = lane_e == ik
        gk = jnp.sum(jnp.where(hit, scores, 0.0), axis=1, keepdims=True)
        sel = jnp.where(hit, -jnp.inf, sel)
        gate_acc = jnp.where(lane_o == k, gk, gate_acc)
        gsum = gsum + gk
        chosen = jnp.where(hit, 1.0, chosen)
        picks.append((ik, hit))
    gate_ref[...] = gate_acc / gsum * ROUTE_SCALE

    r = lax.broadcasted_iota(jnp.int32, (tm, tm), 0)
    c = lax.broadcasted_iota(jnp.int32, (tm, tm), 1)
    earlier = jnp.where(c < r, 1.0, 0.0).astype(BF16)
    rank = cnt_scr[...] + _dot(earlier, chosen.astype(BF16))
    cnt_new = cnt_scr[...] + jnp.sum(chosen, axis=0, keepdims=True)
    cnt_scr[...] = cnt_new
    cnt_ref[...] = cnt_new
    dest_acc = jnp.zeros((tm, LANES), F32)
    for k, (ik, hit) in enumerate(picks):
        pos = jnp.sum(jnp.where(hit, rank, 0.0), axis=1, keepdims=True)
        dest_acc = jnp.where(lane_o == k, ik * float(MOE_CAP) + pos, dest_acc)
    dest_i = dest_acc.astype(jnp.int32)
    dest_ref[...] = dest_i
    dest_v[...] = dest_i
    to_smem = pltpu.make_async_copy(dest_v, dest_s, sem_d)
    to_smem.start()
    to_smem.wait()

    slot = lax.rem(i, 2)

    def wait_rows(s):
        for _ in range(TOP_K):
            pltpu.make_async_copy(xpk.at[s], xs_ref.at[pl.ds(0, tm)], sem_rows.at[s]).wait()

    @pl.when(i >= 2)
    def _():
        wait_rows(slot)

    xpk[slot] = _pack_rows(h2)

    def issue(t, carry):
        for k in range(TOP_K):
            pltpu.make_async_copy(xpk.at[slot, pl.ds(t, 1)], xs_ref.at[pl.ds(dest_s[t, k], 1)], sem_rows.at[slot]).start()
        return carry

    lax.fori_loop(0, tm, issue, 0)
    sh_ref[...] = _shared_ffn(h2, swg_ref[...], swu_ref[...], swd_ref[...])

    @pl.when(i == last)
    def _():
        wait_rows(slot)

        @pl.when(i >= 1)
        def _():
            wait_rows(1 - slot)


def _outproj_call(ya, yb, yc, yd, w_out_bf, x, ada, prm, sh_w, cnt_in, layer, rows_per_b, tm, xs_prev=None):
    m = x.shape[0]
    r = ada.shape[1]
    aliased = xs_prev is not None

    def full(arr):
        nd = arr.ndim
        return pl.BlockSpec(arr.shape, lambda i: (0,) * nd)

    def resident(shape):
        return pl.BlockSpec((None,) + shape, lambda i: (layer, 0, 0), pipeline_mode=pl.Buffered(1))

    grp = pl.BlockSpec((tm, W_GROUP), lambda i: (i, 0))
    row = pl.BlockSpec((tm, D_MODEL), lambda i: (i, 0))
    lane_out = pl.BlockSpec((tm, LANES), lambda i: (i, 0))
    cnt_spec = pl.BlockSpec((1, N_EXPERTS), lambda i: (0, 0))

    def mod(colidx):
        return pl.BlockSpec((None, r, D_MODEL), _mod_spec(rows_per_b, tm, colidx))

    small = [prm["ln1_g"], prm["ln1_b"], prm["router_w_bf"], prm["router_bias"]]
    in_specs = ([grp, grp, grp, grp, resident((D_MODEL, D_MODEL)), row, mod(2), mod(4), mod(3)]
                + [full(a) for a in small]
                + [resident((D_MODEL, D_EXPERT)), resident((D_MODEL, D_EXPERT)), resident((D_EXPERT, D_MODEL)), cnt_spec])
    args = [ya, yb, yc, yd, w_out_bf, x, ada, ada, ada, *small, *sh_w, cnt_in]
    if aliased:
        in_specs.append(pl.BlockSpec(memory_space=pl.ANY))
        args.append(xs_prev)
    return pl.pallas_call(
        functools.partial(_outproj_kernel, aliased=aliased),
        out_shape=(jax.ShapeDtypeStruct((m, D_MODEL), F32), jax.ShapeDtypeStruct((m, D_MODEL), F32),
                   jax.ShapeDtypeStruct((m, LANES), F32), jax.ShapeDtypeStruct((m, LANES), jnp.int32),
                   jax.ShapeDtypeStruct((1, N_EXPERTS), F32), jax.ShapeDtypeStruct((MOE_SLOTS, HALF), U32)),
        grid=(m // tm,),
        in_specs=in_specs,
        out_specs=(row, row, lane_out, lane_out, cnt_spec, pl.BlockSpec(memory_space=pl.ANY)),
        scratch_shapes=[pltpu.VMEM((2, tm, HALF), U32), pltpu.VMEM((tm, LANES), jnp.int32),
                        pltpu.SMEM((tm, LANES), jnp.int32), pltpu.VMEM((1, N_EXPERTS), F32),
                        pltpu.SemaphoreType.DMA((2,)), pltpu.SemaphoreType.DMA],
        input_output_aliases={len(args) - 1: 5} if aliased else {},
        compiler_params=_cparams(("arbitrary",)),
        name="outproj_route_dispatch",
    )(*args)


def _expert_plan(counts):
    experts = jnp.arange(N_EXPERTS, dtype=jnp.int32)
    nblk = (counts + MOE_BM - 1) // MOE_BM
    run_end = jnp.cumsum(nblk)
    run_start = run_end - nblk
    bi = jnp.arange(MOE_NB, dtype=jnp.int32)
    e_of = jnp.minimum(jnp.sum((run_end[None, :] <= bi[:, None]).astype(jnp.int32), axis=1), N_EXPERTS - 1)
    onehot = (e_of[:, None] == experts[None, :]).astype(jnp.int32)
    pick = lambda v: jnp.sum(onehot * v[None, :], axis=1)
    b_in = bi - pick(run_start)
    nonempty = nblk > 0
    later = nonempty[None, :] & (experts[None, :] > experts[:, None])
    nxt = jnp.min(jnp.where(later, experts[None, :], N_EXPERTS), axis=1)
    nxt = jnp.where(nxt == N_EXPERTS, -1, nxt)
    run_rank = jnp.cumsum(nonempty.astype(jnp.int32)) - 1
    i32 = lambda v: v.astype(jnp.int32)
    return (i32(e_of), i32(e_of * MOE_CAP_BLKS + b_in), i32(jnp.clip(pick(counts) - b_in * MOE_BM, 0, MOE_BM)),
            i32(b_in == 0), i32(pick(run_rank) & 1), i32(pick(nxt)), i32(run_end[-1:]))


def _expert_kernel(be_ref, bo_ref, nv_ref, bf_ref, bs_ref, bn_ref, nu_ref, x_ref, wg_hbm, wu_hbm, wd_hbm, o_ref,
                   wg_f, wu_f, wd_f, wg_s, wu_s, wd_s, sem, *, layer):
    i = pl.program_id(0)

    def weight_copies(e, slot):
        return (pltpu.make_async_copy(wg_hbm.at[layer, e], wg_f.at[slot], sem.at[slot]),
                pltpu.make_async_copy(wu_hbm.at[layer, e], wu_f.at[slot], sem.at[slot]),
                pltpu.make_async_copy(wd_hbm.at[layer, e], wd_f.at[slot], sem.at[slot]))

    @pl.when(i == 0)
    def _():
        for cp in weight_copies(be_ref[0], 0):
            cp.start()

    @pl.when(i < nu_ref[0])
    def _():
        @pl.when(bf_ref[i] == 1)
        def _():
            slot = bs_ref[i]
            for cp in weight_copies(be_ref[i], slot):
                cp.wait()
            nxt = bn_ref[i]

            @pl.when(nxt >= 0)
            def _():
                for cp in weight_copies(nxt, 1 - slot):
                    cp.start()

            wg_s[...] = wg_f[slot].astype(BF16)
            wu_s[...] = wu_f[slot].astype(BF16)
            wd_s[...] = wd_f[slot].astype(BF16)

        rows = lax.broadcasted_iota(jnp.int32, (MOE_BM, 1), 0)
        a, b = _unpack_rows(jnp.where(rows < nv_ref[i], x_ref[...], U32(0)))
        x = jnp.concatenate([a.astype(BF16), b.astype(BF16)], axis=1)
        g = _dot(x, wg_s[...])
        u = _dot(x, wu_s[...])
        o_ref[...] = _pack_rows(_dot((_silu(g) * u).astype(BF16), wd_s[...]))


def _expert_call(plan, xs, w_gate, w_up, w_down, layer):
    def rows_map(i, be, bo, nv, bf, bs, bn, nu):
        return (bo[jnp.minimum(i, nu[0] - 1)], 0)

    rows = pl.BlockSpec((MOE_BM, HALF), rows_map)
    hbm = pl.BlockSpec(memory_space=pl.ANY)
    grid_spec = pltpu.PrefetchScalarGridSpec(
        num_scalar_prefetch=7,
        grid=(MOE_NB,),
        in_specs=[rows, hbm, hbm, hbm],
        out_specs=rows,
        scratch_shapes=[pltpu.VMEM((2, D_MODEL, D_EXPERT), F32), pltpu.VMEM((2, D_MODEL, D_EXPERT), F32),
                        pltpu.VMEM((2, D_EXPERT, D_MODEL), F32),
                        pltpu.VMEM((D_MODEL, D_EXPERT), BF16), pltpu.VMEM((D_MODEL, D_EXPERT), BF16),
                        pltpu.VMEM((D_EXPERT, D_MODEL), BF16), pltpu.SemaphoreType.DMA((2,))],
    )
    return pl.pallas_call(
        functools.partial(_expert_kernel, layer=layer),
        out_shape=jax.ShapeDtypeStruct((MOE_SLOTS, HALF), U32),
        grid_spec=grid_spec,
        compiler_params=_cparams(("arbitrary",)),
        name="moe_experts",
    )(*plan, xs, w_gate, w_up, w_down)


def _combine_kernel(dest_ref, rows_ref, gate_ref, sh_ref, x1_ref, g2_ref, lg_ref, lb_ref, o_ref, gbuf, sem):
    tm = x1_ref.shape[0]
    i = pl.program_id(0)
    slot = lax.rem(i, 2)

    def issue(step, s):
        base = step * tm

        def body(t, carry):
            for k in range(TOP_K):
                d = dest_ref[(base + t) * TOP_K + k]
                pltpu.make_async_copy(rows_ref.at[pl.ds(d, 1)], gbuf.at[s, k, pl.ds(t, 1)], sem.at[s]).start()
            return carry

        lax.fori_loop(0, tm, body, 0)

    @pl.when(i == 0)
    def _():
        issue(0, 0)

    @pl.when(i + 1 < pl.num_programs(0))
    def _():
        issue(i + 1, 1 - slot)

    for k in range(TOP_K):
        pltpu.make_async_copy(rows_ref.at[pl.ds(0, tm)], gbuf.at[slot, k], sem.at[slot]).wait()
    gate = gate_ref[...]
    sh = sh_ref[...]
    lo = sh[:, :HALF]
    hi = sh[:, HALF:]
    for k in range(TOP_K):
        a, b = _unpack_rows(gbuf[slot, k])
        gk = gate[:, k:k + 1]
        lo = lo + gk * a
        hi = hi + gk * b
    moe = jnp.concatenate([lo, hi], axis=1)
    o_ref[...] = _layer_norm(ALPHA * x1_ref[...] + g2_ref[...] * moe, lg_ref[...], lb_ref[...])


def _combine_call(dest, out_rows, gate, sh, x1, ada, prm, rows_per_b, tm):
    m = x1.shape[0]
    r = ada.shape[1]
    row = pl.BlockSpec((tm, D_MODEL), lambda i, d: (i, 0))
    vec = pl.BlockSpec((1, D_MODEL), lambda i, d: (0, 0))
    grid_spec = pltpu.PrefetchScalarGridSpec(
        num_scalar_prefetch=1,
        grid=(m // tm,),
        in_specs=[pl.BlockSpec(memory_space=pl.ANY),
                  pl.BlockSpec((tm, LANES), lambda i, d: (i, 0)), row, row,
                  pl.BlockSpec((None, r, D_MODEL), _mod_spec(rows_per_b, tm, 5)), vec, vec],
        out_specs=row,
        scratch_shapes=[pltpu.VMEM((2, TOP_K, tm, HALF), U32), pltpu.SemaphoreType.DMA((2,))],
    )
    return pl.pallas_call(
        _combine_kernel,
        out_shape=jax.ShapeDtypeStruct((m, D_MODEL), F32),
        grid_spec=grid_spec,
        compiler_params=_cparams(("arbitrary",)),
        name="moe_combine",
    )(dest, out_rows, gate, sh, x1, ada, prm["ln2_g"], prm["ln2_b"])


def _block_diag(w):
    eye = jnp.eye(LRU_BLOCKS, dtype=w.dtype)
    full = jnp.einsum("hcd,hg->hcgd", w, eye).reshape(W_GROUP, W_GROUP)
    return full.astype(BF16)


def _rope_tables(pos):
    half = HEAD_DIM // 2
    freqs = ROPE_THETA ** (-jnp.arange(half, dtype=F32) / half)
    ang = pos.astype(F32)[:, None] * freqs[None, :]
    cos = jnp.cos(ang)
    sin = jnp.sin(ang)
    cos_t = jnp.concatenate([cos, cos, cos, cos], axis=1)
    sin_t = jnp.concatenate([-sin, sin, -sin, sin], axis=1)
    return cos_t, sin_t


def kernel(x_prompt, x_sample, cache_k, cache_v, state_conv_a, state_lru, state_conv_c, state_conv_d, page_table, c_prompt, c_sample, w_ada, b_ada, w_in, conv_a_w, conv_a_b, lru_wr, lru_br, lru_wi, lru_bi, lru_lambda, lam_q1, lam_k1, lam_q2, lam_k2, subln_g, conv_c_w, conv_c_b, ln_c_g, ln_c_b, conv_d_w, w_out, ln1_g, ln1_b, ln2_g, ln2_b, router_w, router_bias, exp_w_gate, exp_w_up, exp_w_down, sh_w_gate, sh_w_up, sh_w_down):
    n_pool = cache_k.shape[1]
    c_all = jnp.concatenate([c_prompt, c_sample, jnp.zeros((16 - BATCH - DEC_BATCH, D_MODEL), F32)], axis=0)
    ada_all = _ada_call(c_all, w_ada, b_ada)
    w_in_bf = w_in.astype(BF16)
    w_out_bf = w_out.astype(BF16)
    sh_wg_bf = sh_w_gate.astype(BF16)
    sh_wu_bf = sh_w_up.astype(BF16)
    sh_wd_bf = sh_w_down.astype(BF16)
    sh_w = (sh_wg_bf, sh_wu_bf, sh_wd_bf)
    cnt0 = jnp.zeros((1, N_EXPERTS), F32)
    cache_k4 = cache_k.reshape(DEPTH, n_pool, PAGE_SIZE * N_HEADS, LANES)
    cache_v4 = cache_v.reshape(DEPTH, n_pool, PAGE_SIZE * N_HEADS, LANES)
    cos_p, sin_p = _rope_tables(jnp.arange(SEQ, dtype=jnp.int32))
    cos_s, sin_s = _rope_tables(jnp.full((1,), PAST_LEN, jnp.int32))
    lane = jnp.arange(LANES)
    comp_mask = jnp.stack([lane < HEAD_DIM, lane >= HEAD_DIM]).astype(F32)

    xp = x_prompt.reshape(T_PROMPT, D_MODEL)
    xs = x_sample.reshape(DEC_BATCH, D_MODEL)
    outs = [[] for _ in range(12)]
    for l in range(DEPTH):
        lam_init = 0.8 - 0.6 * math.exp(-0.3 * l)
        prm = dict(
            conv_a_w=conv_a_w[l], conv_a_b=conv_a_b[l][None], wr=_block_diag(lru_wr[l]), br=lru_br[l][None],
            wi=_block_diag(lru_wi[l]), bi=lru_bi[l][None], lam=lru_lambda[l][None],
            conv_c_w=conv_c_w[l], conv_c_b=conv_c_b[l][None], ln_c_g=ln_c_g[l][None], ln_c_b=ln_c_b[l][None],
            conv_d_w=conv_d_w[l],
            lam_q1=lam_q1[l][None], lam_k1=lam_k1[l][None], lam_q2=lam_q2[l][None], lam_k2=lam_k2[l][None],
            subln_g=subln_g[l][None], subln_g_col=subln_g[l][:, None], lam_init=jnp.full((1, 1), lam_init, F32),
            ln1_g=ln1_g[l][None], ln1_b=ln1_b[l][None], ln2_g=ln2_g[l][None], ln2_b=ln2_b[l][None],
            router_w_bf=router_w[l].astype(BF16), router_bias=router_bias[l][None],
        )
        ada_p = ada_all[l, :BATCH].reshape(BATCH, 1, 6 * D_MODEL)
        ada_s = ada_all[l, BATCH:BATCH + DEC_BATCH].reshape(1, DEC_BATCH, 6 * D_MODEL)

        proj_p = _inproj_call(xp, ada_p, w_in_bf, l, SEQ, 1024, 512).reshape(BATCH, SEQ, D_IN)
        ya, yc, yd, q_p, k_p, sa, sh, sc, sd = _mixer_prompt_call(proj_p, cos_p, sin_p, prm)
        yb = _flash_call(q_p, k_p, proj_p, prm)
        outs[0].append(k_p)
        outs[1].append(proj_p[:, :, 4 * W_GROUP:5 * W_GROUP])
        outs[2].append(sa)
        outs[3].append(sh.reshape(BATCH, W_GROUP))
        outs[4].append(sc)
        outs[5].append(sd)

        proj_s = _inproj_call(xs, ada_s, w_in_bf, l, DEC_BATCH, DEC_BATCH, 512)
        ya_s, yc_s, yd_s, q_s, k_s, sa_s, sh_s, sc_s, sd_s = _mixer_sample_call(
            proj_s, cos_s, sin_s, prm,
            state_conv_a[l].transpose(1, 0, 2), state_lru[l], state_conv_c[l].transpose(1, 0, 2),
            state_conv_d[l].transpose(1, 0, 2))
        v_s = proj_s[:, 4 * W_GROUP:5 * W_GROUP]
        q4 = q_s.reshape(DEC_BATCH, 1, N_HEADS, LANES) * (HEAD_DIM ** -0.5)
        q8 = (q4 * comp_mask[None, :, None, :]).reshape(DEC_BATCH, 2 * N_HEADS, LANES).astype(BF16)
        pad4 = jnp.zeros((DEC_BATCH, N_HEADS, LANES), F32)
        k_new8 = jnp.concatenate([k_s.reshape(DEC_BATCH, N_HEADS, LANES), pad4], axis=1)
        v_new8 = jnp.concatenate([v_s.reshape(DEC_BATCH, N_HEADS, LANES), pad4], axis=1)
        yb_s = _decode_call(q8, k_new8, v_new8, cache_k4, cache_v4, page_table, prm, l)
        yb_s = yb_s.reshape(DEC_BATCH, W_GROUP).astype(BF16)
        outs[6].append(k_s.reshape(DEC_BATCH, 1, N_HEADS, 2 * HEAD_DIM))
        outs[7].append(v_s.reshape(DEC_BATCH, 1, N_HEADS, 2 * HEAD_DIM))
        outs[8].append(sa_s.transpose(1, 0, 2))
        outs[9].append(sh_s)
        outs[10].append(sc_s.transpose(1, 0, 2))
        outs[11].append(sd_s.transpose(1, 0, 2))

        flat = lambda t: t.reshape(T_PROMPT, W_GROUP)
        x1_p, shared_p, gate_p, dest_p, cnt_p, x_sorted = _outproj_call(
            flat(ya), flat(yb), flat(yc), flat(yd), w_out_bf, xp, ada_p, prm, sh_w, cnt0, l, SEQ, 256)
        x1_s, shared_s, gate_s, dest_s, cnt_all, x_sorted = _outproj_call(
            ya_s, yb_s, yc_s, yd_s, w_out_bf, xs, ada_s, prm, sh_w, cnt_p, l, DEC_BATCH, DEC_BATCH, xs_prev=x_sorted)

        plan = _expert_plan(cnt_all[0].astype(jnp.int32))
        out_rows = _expert_call(plan, x_sorted, exp_w_gate, exp_w_up, exp_w_down, l)
        xp = _combine_call(dest_p[:, :TOP_K].reshape(-1), out_rows, gate_p, shared_p, x1_p, ada_p, prm, SEQ, 128)
        xs = _combine_call(dest_s[:, :TOP_K].reshape(-1), out_rows, gate_s, shared_s, x1_s, ada_s, prm, DEC_BATCH, DEC_BATCH)

    st = [jnp.stack(o, axis=0) for o in outs]
    kv_shape = (DEPTH, BATCH, SEQ // PAGE_SIZE, PAGE_SIZE, N_HEADS, 2 * HEAD_DIM)
    return (xp.reshape(BATCH, SEQ, D_MODEL), xs.reshape(DEC_BATCH, 1, D_MODEL),
            st[0].reshape(kv_shape), st[1].reshape(kv_shape), st[2], st[3], st[4], st[5],
            st[6], st[7], st[8], st[9], st[10], st[11])
```

```python
import functools
import math

import jax
import jax.numpy as jnp
from jax import lax
from jax.experimental import pallas as pl
from jax.experimental.pallas import tpu as pltpu

F32 = jnp.float32
BF16 = jnp.bfloat16

D_MODEL = 2048
BATCH = 2
SEQ = 4096
DEPTH = 4
DEC_BATCH = 8
PAST_LEN = 16384
PAGE_SIZE = 128
W_GROUP = 512
N_IN_PARTS = 10
D_IN = N_IN_PARTS * W_GROUP
CONV_A = 4
LRU_BLOCKS = 8
LRU_BLOCK_DIM = W_GROUP // LRU_BLOCKS
LRU_C = 8.0
HEAD_DIM = 64
N_HEADS = 4
ROPE_THETA = 10000.0
CONV_C = 31
CONV_D = 3
N_EXPERTS = 64
TOP_K = 8
D_EXPERT = 512
ROUTE_SCALE = 2.5
LN_EPS = 1e-5
ALPHA = (2.0 * DEPTH) ** 0.25
NEG_INF = -1e30

LANES = 128
VMEM_LIMIT = 56 * 1024 * 1024

T_PROMPT = BATCH * SEQ
T_ALL = T_PROMPT + DEC_BATCH
MOE_BM = 256
MOE_NB = -(-T_ALL * TOP_K // MOE_BM) + N_EXPERTS
N_PAGES = PAST_LEN // PAGE_SIZE
DEC_G = 8


def _cparams(sem, **kw):
    return pltpu.CompilerParams(dimension_semantics=sem, vmem_limit_bytes=VMEM_LIMIT, **kw)


def _dot(a, b):
    return jnp.dot(a, b, preferred_element_type=F32)


def _dot_nt(a, b):
    return lax.dot_general(a, b, (((1,), (1,)), ((), ())), preferred_element_type=F32)


def _sigmoid(x):
    return 1.0 / (1.0 + jnp.exp(-x))


def _silu(x):
    return x * _sigmoid(x)


def _layer_norm(z, g, b):
    mu = jnp.mean(z, axis=-1, keepdims=True)
    zc = z - mu
    var = jnp.mean(zc * zc, axis=-1, keepdims=True)
    return zc * lax.rsqrt(var + LN_EPS) * g + b


def _ada_kernel(c_ref, w_ref, b_ref, o_ref):
    c = c_ref[...]
    o_ref[...] = _dot(_silu(c).astype(BF16), w_ref[...].astype(BF16)) + b_ref[...]


def _ada_call(c_all, w_ada, b_ada):
    r = c_all.shape[0]
    tn = 1024
    return pl.pallas_call(
        _ada_kernel,
        out_shape=jax.ShapeDtypeStruct((DEPTH, r, 6 * D_MODEL), F32),
        grid=(DEPTH, 6 * D_MODEL // tn),
        in_specs=[
            pl.BlockSpec((r, D_MODEL), lambda l, n: (0, 0)),
            pl.BlockSpec((None, D_MODEL, tn), lambda l, n: (l, 0, n)),
            pl.BlockSpec((None, 1, tn), lambda l, n: (l, 0, n)),
        ],
        out_specs=pl.BlockSpec((None, r, tn), lambda l, n: (l, 0, n)),
        compiler_params=_cparams(("arbitrary", "arbitrary")),
        name="ada",
    )(c_all, w_ada, b_ada.reshape(DEPTH, 1, 6 * D_MODEL))


def _inproj_kernel(x_ref, sc_ref, sh_ref, w_ref, o_ref, h_scr):
    @pl.when(pl.program_id(1) == 0)
    def _():
        h_scr[...] = (x_ref[...] * (1.0 + sc_ref[...]) + sh_ref[...]).astype(BF16)

    o_ref[...] = _dot(h_scr[...], w_ref[...])


def _mod_spec(rows_per_b, tm, col):
    def imap(m, *_):
        return ((m * tm) // rows_per_b, 0, col)
    return imap


def _inproj_call(x, ada, w_in_bf, layer, rows_per_b, tm, tn):
    m = x.shape[0]
    r = ada.shape[1]
    return pl.pallas_call(
        _inproj_kernel,
        out_shape=jax.ShapeDtypeStruct((m, D_IN), F32),
        grid=(m // tm, D_IN // tn),
        in_specs=[
            pl.BlockSpec((tm, D_MODEL), lambda i, n: (i, 0)),
            pl.BlockSpec((None, r, D_MODEL), _mod_spec(rows_per_b, tm, 1)),
            pl.BlockSpec((None, r, D_MODEL), _mod_spec(rows_per_b, tm, 0)),
            pl.BlockSpec((None, D_MODEL, tn), lambda i, n: (layer, 0, n)),
        ],
        out_specs=pl.BlockSpec((tm, tn), lambda i, n: (i, n)),
        scratch_shapes=[pltpu.VMEM((tm, D_MODEL), BF16)],
        compiler_params=_cparams(("arbitrary", "arbitrary")),
        name="inproj",
    )(x, ada, ada, w_in_bf)


def _softplus(z):
    return jnp.maximum(z, 0.0) + jnp.log1p(jnp.exp(-jnp.abs(z)))


def _gelu_tanh(x):
    return 0.5 * x * (1.0 + jnp.tanh(math.sqrt(2.0 / math.pi) * (x + 0.044715 * (x * x * x))))


def _lru_gates(xa, wr, br, wi, bi, lam):
    xb = xa.astype(BF16)
    r = _sigmoid(_dot(xb, wr) + br)
    i = _sigmoid(_dot(xb, wi) + bi)
    log_a = -LRU_C * r * _softplus(-lam)
    a = jnp.exp(log_a)
    u = jnp.sqrt(jnp.tanh(-log_a) * (1.0 + a * a)) * (i * xa)
    return a, u


def _rope(x, cos_t, sin_t):
    lane = lax.broadcasted_iota(jnp.int32, x.shape, 1)
    first_half = (lane & (HEAD_DIM // 2)) == 0
    w = x.shape[1]
    partner = jnp.where(first_half, pltpu.roll(x, w - HEAD_DIM // 2, axis=1), pltpu.roll(x, HEAD_DIM // 2, axis=1))
    return x * cos_t + partner * sin_t


def _tile4(t):
    return jnp.concatenate([t, t, t, t], axis=1)


def _conv_from_buf(buf_ref, w_ref, hist, n_taps, rows, rb):
    outs = []
    for r0 in range(0, rows, rb):
        acc = None
        for j in range(n_taps):
            term = buf_ref[pl.ds(hist - (n_taps - 1) + j + r0, rb), :] * w_ref[pl.ds(j, 1), :]
            acc = term if acc is None else acc + term
        outs.append(acc)
    return jnp.concatenate(outs, axis=0) if len(outs) > 1 else outs[0]


def _linear_scan(a, b):
    n = a.shape[0]
    rows = lax.broadcasted_iota(jnp.int32, a.shape, 0)
    d = 1
    while d < n:
        keep = rows >= d
        a_sh = jnp.where(keep, pltpu.roll(a, d, axis=0), 1.0)
        b_sh = jnp.where(keep, pltpu.roll(b, d, axis=0), 0.0)
        b = a * b_sh + b
        a = a * a_sh
        d *= 2
    return a, b


MIX_TC = 256
HIST_A = 8
HIST_C = 32
HIST_D = 8
CONV_RB = 32


def _mixer_prompt_kernel(ax_ref, ag_ref, bq_ref, bk_ref, cv_ref, cg_ref, db_ref, dc_ref, dx_ref,
                         cos_ref, sin_ref, caw_ref, cab_ref, wr_ref, br_ref, wi_ref, bi_ref, lam_ref,
                         ccw_ref, ccb_ref, lng_ref, lnb_ref, cdw_ref,
                         ya_ref, yc_ref, yd_ref, q_ref, k_ref, sa_ref, sh_ref, sc_ref, sd_ref,
                         xa_buf, uc_buf, dd_buf, h_buf):
    tc = MIX_TC

    @pl.when(pl.program_id(1) == 0)
    def _():
        xa_buf[pl.ds(0, HIST_A), :] = jnp.zeros((HIST_A, W_GROUP), F32)
        uc_buf[pl.ds(0, HIST_C), :] = jnp.zeros((HIST_C, W_GROUP), F32)
        dd_buf[pl.ds(0, HIST_D), :] = jnp.zeros((HIST_D, W_GROUP), F32)
        h_buf[...] = jnp.zeros((1, W_GROUP), F32)

    xa_buf[pl.ds(HIST_A, tc), :] = ax_ref[...]
    xa = _conv_from_buf(xa_buf, caw_ref, HIST_A, CONV_A, tc, CONV_RB) + cab_ref[...]
    a, u = _lru_gates(xa, wr_ref[...], br_ref[...], wi_ref[...], bi_ref[...], lam_ref[...])
    a_cum, b_cum = _linear_scan(a, u)
    h = a_cum * h_buf[...] + b_cum
    ya_ref[...] = (_gelu_tanh(ag_ref[...]) * h).astype(BF16)
    h_last = h[tc - 1:tc, :]
    h_buf[...] = h_last
    sh_ref[...] = h_last
    sa_ref[...] = xa_buf[pl.ds(HIST_A + tc - (CONV_A - 1), CONV_A - 1), :]
    xa_buf[pl.ds(0, HIST_A), :] = xa_buf[pl.ds(tc, HIST_A), :]

    uc_buf[pl.ds(HIST_C, tc), :] = cv_ref[...] * _sigmoid(cg_ref[...])
    yc = _conv_from_buf(uc_buf, ccw_ref, HIST_C, CONV_C, tc, CONV_RB) + ccb_ref[...]
    yc_ref[...] = _silu(_layer_norm(yc, lng_ref[...], lnb_ref[...])).astype(BF16)
    sc_ref[...] = uc_buf[pl.ds(HIST_C + tc - (CONV_C - 1), CONV_C - 1), :]
    uc_buf[pl.ds(0, HIST_C), :] = uc_buf[pl.ds(tc, HIST_C), :]

    dd_buf[pl.ds(HIST_D, tc), :] = dc_ref[...] * dx_ref[...]
    yd = _conv_from_buf(dd_buf, cdw_ref, HIST_D, CONV_D, tc, CONV_RB)
    yd_ref[...] = (db_ref[...] * yd).astype(BF16)
    sd_ref[...] = dd_buf[pl.ds(HIST_D + tc - (CONV_D - 1), CONV_D - 1), :]
    dd_buf[pl.ds(0, HIST_D), :] = dd_buf[pl.ds(tc, HIST_D), :]

    cos_t = _tile4(cos_ref[...])
    sin_t = _tile4(sin_ref[...])
    q_ref[...] = (_rope(bq_ref[...], cos_t, sin_t) * (HEAD_DIM ** -0.5)).astype(BF16)
    k_ref[...] = _rope(bk_ref[...], cos_t, sin_t)


def _mixer_prompt_call(proj, cos_t, sin_t, prm):
    tc = MIX_TC

    def col(j):
        return pl.BlockSpec((None, tc, W_GROUP), lambda b, c: (b, c, j))

    def full(arr):
        nd = arr.ndim
        return pl.BlockSpec(arr.shape, lambda b, c: (0,) * nd)

    small = [prm["conv_a_w"], prm["conv_a_b"], prm["wr"], prm["br"], prm["wi"], prm["bi"], prm["lam"],
             prm["conv_c_w"], prm["conv_c_b"], prm["ln_c_g"], prm["ln_c_b"], prm["conv_d_w"]]
    tok = pl.BlockSpec((None, tc, W_GROUP), lambda b, c: (b, c, 0))

    def state(n):
        return pl.BlockSpec((None, n, W_GROUP), lambda b, c: (b, 0, 0))

    out_shape = (
        jax.ShapeDtypeStruct((BATCH, SEQ, W_GROUP), BF16),
        jax.ShapeDtypeStruct((BATCH, SEQ, W_GROUP), BF16),
        jax.ShapeDtypeStruct((BATCH, SEQ, W_GROUP), BF16),
        jax.ShapeDtypeStruct((BATCH, SEQ, W_GROUP), BF16),
        jax.ShapeDtypeStruct((BATCH, SEQ, W_GROUP), F32),
        jax.ShapeDtypeStruct((BATCH, CONV_A - 1, W_GROUP), F32),
        jax.ShapeDtypeStruct((BATCH, 1, W_GROUP), F32),
        jax.ShapeDtypeStruct((BATCH, CONV_C - 1, W_GROUP), F32),
        jax.ShapeDtypeStruct((BATCH, CONV_D - 1, W_GROUP), F32),
    )
    return pl.pallas_call(
        _mixer_prompt_kernel,
        out_shape=out_shape,
        grid=(BATCH, SEQ // tc),
        in_specs=[col(0), col(1), col(2), col(3), col(5), col(6), col(7), col(8), col(9),
                  pl.BlockSpec((tc, LANES), lambda b, c: (c, 0)),
                  pl.BlockSpec((tc, LANES), lambda b, c: (c, 0))] + [full(a) for a in small],
        out_specs=(tok, tok, tok, tok, tok, state(CONV_A - 1), state(1), state(CONV_C - 1), state(CONV_D - 1)),
        scratch_shapes=[pltpu.VMEM((HIST_A + tc, W_GROUP), F32), pltpu.VMEM((HIST_C + tc, W_GROUP), F32),
                        pltpu.VMEM((HIST_D + tc, W_GROUP), F32), pltpu.VMEM((1, W_GROUP), F32)],
        compiler_params=_cparams(("arbitrary", "arbitrary")),
        name="mixer_prompt",
    )(*([proj] * 9), cos_t, sin_t, *small)


def _mixer_sample_kernel(ax_ref, ag_ref, bq_ref, bk_ref, cv_ref, cg_ref, db_ref, dc_ref, dx_ref,
                         cos_ref, sin_ref, caw_ref, cab_ref, wr_ref, br_ref, wi_ref, bi_ref, lam_ref,
                         ccw_ref, ccb_ref, lng_ref, lnb_ref, cdw_ref,
                         ba_ref, h0_ref, bc_ref, bd_ref,
                         ya_ref, yc_ref, yd_ref, q_ref, k_ref, sa_ref, sh_ref, sc_ref, sd_ref):
    ax = ax_ref[...]
    xa = cab_ref[...] + caw_ref[pl.ds(CONV_A - 1, 1), :] * ax
    for j in range(CONV_A - 1):
        xa = xa + caw_ref[pl.ds(j, 1), :] * ba_ref[j]
    a, u = _lru_gates(xa, wr_ref[...], br_ref[...], wi_ref[...], bi_ref[...], lam_ref[...])
    h = a * h0_ref[...] + u
    ya_ref[...] = (_gelu_tanh(ag_ref[...]) * h).astype(BF16)
    sh_ref[...] = h
    for j in range(CONV_A - 2):
        sa_ref[j] = ba_ref[j + 1]
    sa_ref[CONV_A - 2] = ax

    uc = cv_ref[...] * _sigmoid(cg_ref[...])
    yc = ccb_ref[...] + ccw_ref[pl.ds(CONV_C - 1, 1), :] * uc
    for j in range(CONV_C - 1):
        yc = yc + ccw_ref[pl.ds(j, 1), :] * bc_ref[j]
    yc_ref[...] = _silu(_layer_norm(yc, lng_ref[...], lnb_ref[...])).astype(BF16)
    for j in range(CONV_C - 2):
        sc_ref[j] = bc_ref[j + 1]
    sc_ref[CONV_C - 2] = uc

    dd = dc_ref[...] * dx_ref[...]
    yd = cdw_ref[pl.ds(CONV_D - 1, 1), :] * dd
    for j in range(CONV_D - 1):
        yd = yd + cdw_ref[pl.ds(j, 1), :] * bd_ref[j]
    yd_ref[...] = (db_ref[...] * yd).astype(BF16)
    for j in range(CONV_D - 2):
        sd_ref[j] = bd_ref[j + 1]
    sd_ref[CONV_D - 2] = dd

    cos_t = _tile4(cos_ref[...])
    sin_t = _tile4(sin_ref[...])
    q_ref[...] = _rope(bq_ref[...], cos_t, sin_t)
    k_ref[...] = _rope(bk_ref[...], cos_t, sin_t)


def _mixer_sample_call(proj, cos_t, sin_t, prm, buf_a, h0, buf_c, buf_d):
    nb = DEC_BATCH

    def col(j):
        return pl.BlockSpec((nb, W_GROUP), lambda i: (0, j))

    def full(arr):
        nd = arr.ndim
        return pl.BlockSpec(arr.shape, lambda i: (0,) * nd)

    small = [prm["conv_a_w"], prm["conv_a_b"], prm["wr"], prm["br"], prm["wi"], prm["bi"], prm["lam"],
             prm["conv_c_w"], prm["conv_c_b"], prm["ln_c_g"], prm["ln_c_b"], prm["conv_d_w"]]
    states = [buf_a, h0, buf_c, buf_d]
    tok_bf = jax.ShapeDtypeStruct((nb, W_GROUP), BF16)
    tok_f = jax.ShapeDtypeStruct((nb, W_GROUP), F32)
    out_shape = (tok_bf, tok_bf, tok_bf, tok_f, tok_f,
                 jax.ShapeDtypeStruct(buf_a.shape, F32), tok_f,
                 jax.ShapeDtypeStruct(buf_c.shape, F32), jax.ShapeDtypeStruct(buf_d.shape, F32))
    return pl.pallas_call(
        _mixer_sample_kernel,
        out_shape=out_shape,
        grid=(1,),
        in_specs=[col(0), col(1), col(2), col(3), col(5), col(6), col(7), col(8), col(9),
                  full(cos_t), full(sin_t)] + [full(a) for a in small] + [full(a) for a in states],
        out_specs=tuple(pl.BlockSpec(s.shape, lambda i, nd=len(s.shape): (0,) * nd) for s in out_shape),
        compiler_params=_cparams(("arbitrary",)),
        name="mixer_sample",
    )(*([proj] * 9), cos_t, sin_t, *small, *states)


def _diff_lambda(lq1_ref, lk1_ref, lq2_ref, lk2_ref, lam_init):
    s1 = jnp.sum(lq1_ref[...] * lk1_ref[...], axis=-1, keepdims=True)
    s2 = jnp.sum(lq2_ref[...] * lk2_ref[...], axis=-1, keepdims=True)
    return jnp.exp(s1) - jnp.exp(s2) + lam_init


def _diff_finish(o1, o2, lam, g, lam_init):
    od = o1 - lam * o2
    ms = jnp.mean(od * od, axis=-1, keepdims=True)
    return od * lax.rsqrt(ms + LN_EPS) * g * (1.0 - lam_init)


FLASH_T = 512


def _flash_kernel(q_ref, k_ref, v_ref, lq1_ref, lk1_ref, lq2_ref, lk2_ref, gcol_ref, li_ref, o_ref,
                  kb_scr, vt_scr, m_scr, l_scr, acc_scr):
    t = FLASH_T
    i = pl.program_id(2)

    @pl.when(i == 0)
    def _():
        for c in range(SEQ // t):
            kb_scr[c] = k_ref[pl.ds(c * t, t), :].astype(BF16)
            vt_scr[c] = v_ref[pl.ds(c * t, t), :].T.astype(BF16)

    q = q_ref[...]
    lane = lax.broadcasted_iota(jnp.int32, q.shape, 1)
    zero = jnp.zeros_like(q)
    qq = jnp.concatenate([jnp.where(lane < HEAD_DIM, q, zero), jnp.where(lane >= HEAD_DIM, q, zero)], axis=0)
    m_scr[...] = jnp.full(m_scr.shape, NEG_INF, F32)
    l_scr[...] = jnp.zeros(l_scr.shape, F32)
    acc_scr[...] = jnp.zeros(acc_scr.shape, F32)

    def step(j, masked):
        st = _dot_nt(kb_scr[j], qq)
        if masked:
            r = lax.broadcasted_iota(jnp.int32, st.shape, 0)
            c = lax.broadcasted_iota(jnp.int32, st.shape, 1)
            st = jnp.where(r <= jnp.where(c >= t, c - t, c), st, NEG_INF)
        m_prev = m_scr[...]
        m_new = jnp.maximum(m_prev, jnp.max(st, axis=0, keepdims=True))
        alpha = jnp.exp(m_prev - m_new)
        p = jnp.exp(st - m_new)
        l_scr[...] = alpha * l_scr[...] + jnp.sum(p, axis=0, keepdims=True)
        acc_scr[...] = alpha * acc_scr[...] + _dot(vt_scr[j], p.astype(BF16))
        m_scr[...] = m_new

    def body(j, carry):
        step(j, False)
        return carry

    lax.fori_loop(0, i, body, 0)
    step(i, True)

    o = acc_scr[...] / l_scr[...]
    lam_init = li_ref[...]
    lam = _diff_lambda(lq1_ref, lk1_ref, lq2_ref, lk2_ref, lam_init)
    od = o[:, :t] - lam * o[:, t:]
    ms = jnp.mean(od * od, axis=0, keepdims=True)
    y = od * lax.rsqrt(ms + LN_EPS) * gcol_ref[...] * (1.0 - lam_init)
    o_ref[...] = y.T.astype(BF16)


def _flash_call(q, k, proj, prm):
    t = FLASH_T
    v_col0 = 4 * W_GROUP // LANES

    def full(arr):
        nd = arr.ndim
        return pl.BlockSpec(arr.shape, lambda b, h, i: (0,) * nd)

    small = [prm["lam_q1"], prm["lam_k1"], prm["lam_q2"], prm["lam_k2"], prm["subln_g_col"], prm["lam_init"]]
    return pl.pallas_call(
        _flash_kernel,
        out_shape=jax.ShapeDtypeStruct((BATCH, SEQ, W_GROUP), BF16),
        grid=(BATCH, N_HEADS, SEQ // t),
        in_specs=[pl.BlockSpec((None, t, LANES), lambda b, h, i: (b, i, h)),
                  pl.BlockSpec((None, SEQ, LANES), lambda b, h, i: (b, 0, h)),
                  pl.BlockSpec((None, SEQ, LANES), lambda b, h, i: (b, 0, v_col0 + h))] + [full(a) for a in small],
        out_specs=pl.BlockSpec((None, t, LANES), lambda b, h, i: (b, i, h)),
        scratch_shapes=[pltpu.VMEM((SEQ // t, t, LANES), BF16), pltpu.VMEM((SEQ // t, LANES, t), BF16),
                        pltpu.VMEM((1, 2 * t), F32), pltpu.VMEM((1, 2 * t), F32), pltpu.VMEM((LANES, 2 * t), F32)],
        compiler_params=_cparams(("arbitrary", "arbitrary", "arbitrary")),
        name="flash",
    )(q, k, proj, *small)


def _softmax_update(s, v, m_scr, l_scr, acc_scr):
    m_prev = m_scr[...]
    m_new = jnp.maximum(m_prev, jnp.max(s, axis=1, keepdims=True))
    alpha = jnp.exp(m_prev - m_new)
    p = jnp.exp(s - m_new)
    l_scr[...] = alpha * l_scr[...] + jnp.sum(p, axis=1, keepdims=True)
    acc_scr[...] = alpha * acc_scr[...] + _dot(p.astype(BF16), v)
    m_scr[...] = m_new


def _decode_kernel(pt_ref, q_ref, *refs):
    k_refs = refs[:DEC_G]
    v_refs = refs[DEC_G:2 * DEC_G]
    (kn_ref, vn_ref, lq1_ref, lk1_ref, lq2_ref, lk2_ref, g_ref, li_ref, o_ref, m_scr, l_scr, acc_scr) = refs[2 * DEC_G:]
    s_idx = pl.program_id(1)

    @pl.when(s_idx == 0)
    def _():
        m_scr[...] = jnp.full(m_scr.shape, NEG_INF, F32)
        l_scr[...] = jnp.zeros(l_scr.shape, F32)
        acc_scr[...] = jnp.zeros(acc_scr.shape, F32)

    q = q_ref[...]
    n_kv = PAGE_SIZE * N_HEADS
    row = lax.broadcasted_iota(jnp.int32, (2 * N_HEADS, n_kv), 0)
    col = lax.broadcasted_iota(jnp.int32, (2 * N_HEADS, n_kv), 1)
    same_head = (col & (N_HEADS - 1)) == (row & (N_HEADS - 1))
    s = jnp.concatenate([jnp.where(same_head, _dot_nt(q, k_refs[g][...].astype(BF16)), NEG_INF)
                         for g in range(DEC_G)], axis=1)
    m_prev = m_scr[...]
    m_new = jnp.maximum(m_prev, jnp.max(s, axis=1, keepdims=True))
    alpha = jnp.exp(m_prev - m_new)
    p = jnp.exp(s - m_new)
    l_scr[...] = alpha * l_scr[...] + jnp.sum(p, axis=1, keepdims=True)
    pb = p.astype(BF16)
    pv = _dot(pb[:, :n_kv], v_refs[0][...].astype(BF16))
    for g in range(1, DEC_G):
        pv = pv + _dot(pb[:, g * n_kv:(g + 1) * n_kv], v_refs[g][...].astype(BF16))
    acc_scr[...] = alpha * acc_scr[...] + pv
    m_scr[...] = m_new

    @pl.when(s_idx == pl.num_programs(1) - 1)
    def _():
        kn = kn_ref[...].astype(BF16)
        vn = vn_ref[...].astype(BF16)
        r8 = lax.broadcasted_iota(jnp.int32, (2 * N_HEADS, 2 * N_HEADS), 0)
        c8 = lax.broadcasted_iota(jnp.int32, (2 * N_HEADS, 2 * N_HEADS), 1)
        s = jnp.where(c8 == (r8 & (N_HEADS - 1)), _dot_nt(q, kn), NEG_INF)
        _softmax_update(s, vn, m_scr, l_scr, acc_scr)
        o = acc_scr[...] / l_scr[...]
        lam_init = li_ref[...]
        lam = _diff_lambda(lq1_ref, lk1_ref, lq2_ref, lk2_ref, lam_init)
        o_ref[...] = _diff_finish(o[:N_HEADS], o[N_HEADS:], lam, g_ref[...], lam_init)


def _decode_call(q8, k_new8, v_new8, cache_k4, cache_v4, page_table, prm, layer):
    n_kv = PAGE_SIZE * N_HEADS
    steps = N_PAGES // DEC_G

    def page_spec(g):
        return pl.BlockSpec((None, None, n_kv, LANES), lambda b, s, pt: (layer, pt[b, s * DEC_G + g], 0, 0))

    def full(arr):
        nd = arr.ndim
        return pl.BlockSpec(arr.shape, lambda b, s, pt: (0,) * nd)

    row8 = pl.BlockSpec((None, 2 * N_HEADS, LANES), lambda b, s, pt: (b, 0, 0))
    small = [prm["lam_q1"], prm["lam_k1"], prm["lam_q2"], prm["lam_k2"], prm["subln_g"], prm["lam_init"]]
    grid_spec = pltpu.PrefetchScalarGridSpec(
        num_scalar_prefetch=1,
        grid=(DEC_BATCH, steps),
        in_specs=[row8] + [page_spec(g) for g in range(DEC_G)] + [page_spec(g) for g in range(DEC_G)]
                 + [row8, row8] + [full(a) for a in small],
        out_specs=pl.BlockSpec((None, N_HEADS, LANES), lambda b, s, pt: (b, 0, 0)),
        scratch_shapes=[pltpu.VMEM((2 * N_HEADS, 1), F32), pltpu.VMEM((2 * N_HEADS, 1), F32),
                        pltpu.VMEM((2 * N_HEADS, LANES), F32)],
    )
    return pl.pallas_call(
        _decode_kernel,
        out_shape=jax.ShapeDtypeStruct((DEC_BATCH, N_HEADS, LANES), F32),
        grid_spec=grid_spec,
        compiler_params=_cparams(("arbitrary", "arbitrary")),
        name="decode_attn",
    )(page_table, q8, *([cache_k4] * DEC_G), *([cache_v4] * DEC_G), k_new8, v_new8, *small)


MOE_CAP = -(-T_ALL // MOE_BM) * MOE_BM
MOE_CAP_BLKS = MOE_CAP // MOE_BM
MOE_SLOTS = N_EXPERTS * MOE_CAP
HALF = D_MODEL // 2
U32 = jnp.uint32


def _pack_rows(x):
    a = lax.bitcast_convert_type(x[:, :HALF].astype(BF16).astype(F32), U32)
    b = lax.bitcast_convert_type(x[:, HALF:].astype(BF16).astype(F32), U32)
    return a | (b >> 16)


def _unpack_rows(w):
    a = lax.bitcast_convert_type(w & U32(0xFFFF0000), F32)
    b = lax.bitcast_convert_type(w << 16, F32)
    return a, b


ROW_TILE = HALF // LANES


def _store_row_tiles(ref, packed):
    n = packed.shape[0]
    for s in range(ROW_TILE):
        ref[pl.ds(s, n, stride=ROW_TILE), :] = packed[:, s * LANES:(s + 1) * LANES]


def _load_row_tiles(ref, n):
    return jnp.concatenate([ref[pl.ds(s, n, stride=ROW_TILE), :] for s in range(ROW_TILE)], axis=1)


def _tile_rows(i):
    return pl.ds(pl.multiple_of(i * ROW_TILE, ROW_TILE), ROW_TILE)


def _shared_ffn(h, wg, wu, wd):
    hb = h.astype(BF16)
    return _dot((_silu(_dot(hb, wg)) * _dot(hb, wu)).astype(BF16), wd)


def _outproj_kernel(ya_ref, yb_ref, yc_ref, yd_ref, w_ref, x_ref, g1_ref, sc2_ref, sh2_ref, lg_ref, lb_ref,
                    rw_ref, rb_ref, swg_ref, swu_ref, swd_ref, cnt_in_ref, *refs, aliased):
    if aliased:
        refs = refs[1:]
    x1_ref, sh_ref, gate_ref, dest_ref, cnt_ref, xs_ref, xpk, dest_v, dest_s, cnt_scr, sem_rows, sem_d = refs
    i = pl.program_id(0)
    last = pl.num_programs(0) - 1
    tm = x_ref.shape[0]

    @pl.when(i == 0)
    def _():
        cnt_scr[...] = cnt_in_ref[...]

    mix = (_dot(ya_ref[...], w_ref[pl.ds(0, W_GROUP), :]) + _dot(yb_ref[...], w_ref[pl.ds(W_GROUP, W_GROUP), :])
           + _dot(yc_ref[...], w_ref[pl.ds(2 * W_GROUP, W_GROUP), :]) + _dot(yd_ref[...], w_ref[pl.ds(3 * W_GROUP, W_GROUP), :]))
    x1 = _layer_norm(ALPHA * x_ref[...] + g1_ref[...] * mix, lg_ref[...], lb_ref[...])
    x1_ref[...] = x1
    h2 = x1 * (1.0 + sc2_ref[...]) + sh2_ref[...]

    scores = _sigmoid(_dot(h2.astype(BF16), rw_ref[...]))
    sel = scores + rb_ref[...]
    lane_e = lax.broadcasted_iota(jnp.int32, (tm, N_EXPERTS), 1).astype(F32)
    lane_o = lax.broadcasted_iota(jnp.int32, (tm, LANES), 1)
    gate_acc = jnp.zeros((tm, LANES), F32)
    gsum = jnp.zeros((tm, 1), F32)
    chosen = jnp.zeros((tm, N_EXPERTS), F32)
    picks = []
    for k in range(TOP_K):
        best = jnp.max(sel, axis=1, keepdims=True)
        ik = jnp.min(jnp.where(sel == best, lane_e, float(N_EXPERTS)), axis=1, keepdims=True)
        hit = lane_e == ik
        gk = jnp.sum(jnp.where(hit, scores, 0.0), axis=1, keepdims=True)
        sel = jnp.where(hit, -jnp.inf, sel)
        gate_acc = jnp.where(lane_o == k, gk, gate_acc)
        gsum = gsum + gk
        chosen = jnp.where(hit, 1.0, chosen)
        picks.append((ik, hit))
    gate_ref[...] = gate_acc / gsum * ROUTE_SCALE

    r = lax.broadcasted_iota(jnp.int32, (tm, tm), 0)
    c = lax.broadcasted_iota(jnp.int32, (tm, tm), 1)
    earlier = jnp.where(c < r, 1.0, 0.0).astype(BF16)
    rank = cnt_scr[...] + _dot(earlier, chosen.astype(BF16))
    cnt_new = cnt_scr[...] + jnp.sum(chosen, axis=0, keepdims=True)
    cnt_scr[...] = cnt_new
    cnt_ref[...] = cnt_new
    dest_acc = jnp.zeros((tm, LANES), F32)
    for k, (ik, hit) in enumerate(picks):
        pos = jnp.sum(jnp.where(hit, rank, 0.0), axis=1, keepdims=True)
        dest_acc = jnp.where(lane_o == k, ik * float(MOE_CAP) + pos, dest_acc)
    dest_i = dest_acc.astype(jnp.int32)
    dest_ref[...] = dest_i
    dest_v[...] = dest_i
    to_smem = pltpu.make_async_copy(dest_v, dest_s, sem_d)
    to_smem.start()
    to_smem.wait()

    slot = lax.rem(i, 2)

    def wait_rows(s):
        for _ in range(TOP_K):
            pltpu.make_async_copy(xpk.at[s], xs_ref.at[pl.ds(0, tm * ROW_TILE)], sem_rows.at[s]).wait()

    @pl.when(i >= 2)
    def _():
        wait_rows(slot)

    _store_row_tiles(xpk.at[slot], _pack_rows(h2))

    def issue(t, carry):
        for k in range(TOP_K):
            pltpu.make_async_copy(xpk.at[slot, _tile_rows(t)], xs_ref.at[_tile_rows(dest_s[t, k])], sem_rows.at[slot]).start()
        return carry

    lax.fori_loop(0, tm, issue, 0)
    sh_ref[...] = _shared_ffn(h2, swg_ref[...], swu_ref[...], swd_ref[...])

    @pl.when(i == last)
    def _():
        wait_rows(slot)

        @pl.when(i >= 1)
        def _():
            wait_rows(1 - slot)


def _outproj_call(ya, yb, yc, yd, w_out_bf, x, ada, prm, sh_w, cnt_in, layer, rows_per_b, tm, xs_prev=None):
    m = x.shape[0]
    r = ada.shape[1]
    aliased = xs_prev is not None

    def full(arr):
        nd = arr.ndim
        return pl.BlockSpec(arr.shape, lambda i: (0,) * nd)

    def resident(shape):
        return pl.BlockSpec((None,) + shape, lambda i: (layer, 0, 0), pipeline_mode=pl.Buffered(1))

    grp = pl.BlockSpec((tm, W_GROUP), lambda i: (i, 0))
    row = pl.BlockSpec((tm, D_MODEL), lambda i: (i, 0))
    lane_out = pl.BlockSpec((tm, LANES), lambda i: (i, 0))
    cnt_spec = pl.BlockSpec((1, N_EXPERTS), lambda i: (0, 0))

    def mod(colidx):
        return pl.BlockSpec((None, r, D_MODEL), _mod_spec(rows_per_b, tm, colidx))

    small = [prm["ln1_g"], prm["ln1_b"], prm["router_w_bf"], prm["router_bias"]]
    in_specs = ([grp, grp, grp, grp, resident((D_MODEL, D_MODEL)), row, mod(2), mod(4), mod(3)]
                + [full(a) for a in small]
                + [resident((D_MODEL, D_EXPERT)), resident((D_MODEL, D_EXPERT)), resident((D_EXPERT, D_MODEL)), cnt_spec])
    args = [ya, yb, yc, yd, w_out_bf, x, ada, ada, ada, *small, *sh_w, cnt_in]
    if aliased:
        in_specs.append(pl.BlockSpec(memory_space=pl.ANY))
        args.append(xs_prev)
    return pl.pallas_call(
        functools.partial(_outproj_kernel, aliased=aliased),
        out_shape=(jax.ShapeDtypeStruct((m, D_MODEL), F32), jax.ShapeDtypeStruct((m, D_MODEL), F32),
                   jax.ShapeDtypeStruct((m, LANES), F32), jax.ShapeDtypeStruct((m, LANES), jnp.int32),
                   jax.ShapeDtypeStruct((1, N_EXPERTS), F32), jax.ShapeDtypeStruct((MOE_SLOTS * ROW_TILE, LANES), U32)),
        grid=(m // tm,),
        in_specs=in_specs,
        out_specs=(row, row, lane_out, lane_out, cnt_spec, pl.BlockSpec(memory_space=pl.ANY)),
        scratch_shapes=[pltpu.VMEM((2, tm * ROW_TILE, LANES), U32), pltpu.VMEM((tm, LANES), jnp.int32),
                        pltpu.SMEM((tm, LANES), jnp.int32), pltpu.VMEM((1, N_EXPERTS), F32),
                        pltpu.SemaphoreType.DMA((2,)), pltpu.SemaphoreType.DMA],
        input_output_aliases={len(args) - 1: 5} if aliased else {},
        compiler_params=_cparams(("arbitrary",)),
        name="outproj_route_dispatch",
    )(*args)


def _expert_plan(counts):
    experts = jnp.arange(N_EXPERTS, dtype=jnp.int32)
    nblk = (counts + MOE_BM - 1) // MOE_BM
    run_end = jnp.cumsum(nblk)
    run_start = run_end - nblk
    bi = jnp.arange(MOE_NB, dtype=jnp.int32)
    e_of = jnp.minimum(jnp.sum((run_end[None, :] <= bi[:, None]).astype(jnp.int32), axis=1), N_EXPERTS - 1)
    onehot = (e_of[:, None] == experts[None, :]).astype(jnp.int32)
    pick = lambda v: jnp.sum(onehot * v[None, :], axis=1)
    b_in = bi - pick(run_start)
    nonempty = nblk > 0
    later = nonempty[None, :] & (experts[None, :] > experts[:, None])
    nxt = jnp.min(jnp.where(later, experts[None, :], N_EXPERTS), axis=1)
    nxt = jnp.where(nxt == N_EXPERTS, -1, nxt)
    run_rank = jnp.cumsum(nonempty.astype(jnp.int32)) - 1
    i32 = lambda v: v.astype(jnp.int32)
    return (i32(e_of), i32(e_of * MOE_CAP_BLKS + b_in), i32(jnp.clip(pick(counts) - b_in * MOE_BM, 0, MOE_BM)),
            i32(b_in == 0), i32(pick(run_rank) & 1), i32(pick(nxt)), i32(run_end[-1:]))


def _expert_kernel(be_ref, bo_ref, nv_ref, bf_ref, bs_ref, bn_ref, nu_ref, x_ref, wg_hbm, wu_hbm, wd_hbm, o_ref,
                   wg_f, wu_f, wd_f, wg_s, wu_s, wd_s, sem, *, layer):
    i = pl.program_id(0)

    def weight_copies(e, slot):
        return (pltpu.make_async_copy(wg_hbm.at[layer, e], wg_f.at[slot], sem.at[slot]),
                pltpu.make_async_copy(wu_hbm.at[layer, e], wu_f.at[slot], sem.at[slot]),
                pltpu.make_async_copy(wd_hbm.at[layer, e], wd_f.at[slot], sem.at[slot]))

    @pl.when(i == 0)
    def _():
        for cp in weight_copies(be_ref[0], 0):
            cp.start()

    @pl.when(i < nu_ref[0])
    def _():
        @pl.when(bf_ref[i] == 1)
        def _():
            slot = bs_ref[i]
            for cp in weight_copies(be_ref[i], slot):
                cp.wait()
            nxt = bn_ref[i]

            @pl.when(nxt >= 0)
            def _():
                for cp in weight_copies(nxt, 1 - slot):
                    cp.start()

            wg_s[...] = wg_f[slot].astype(BF16)
            wu_s[...] = wu_f[slot].astype(BF16)
            wd_s[...] = wd_f[slot].astype(BF16)

        rows = lax.broadcasted_iota(jnp.int32, (MOE_BM, 1), 0)
        a, b = _unpack_rows(jnp.where(rows < nv_ref[i], _load_row_tiles(x_ref, MOE_BM), U32(0)))
        x = jnp.concatenate([a.astype(BF16), b.astype(BF16)], axis=1)
        g = _dot(x, wg_s[...])
        u = _dot(x, wu_s[...])
        _store_row_tiles(o_ref, _pack_rows(_dot((_silu(g) * u).astype(BF16), wd_s[...])))


def _expert_call(plan, xs, w_gate, w_up, w_down, layer):
    def rows_map(i, be, bo, nv, bf, bs, bn, nu):
        return (bo[jnp.minimum(i, nu[0] - 1)], 0)

    rows = pl.BlockSpec((MOE_BM * ROW_TILE, LANES), rows_map)
    hbm = pl.BlockSpec(memory_space=pl.ANY)
    grid_spec = pltpu.PrefetchScalarGridSpec(
        num_scalar_prefetch=7,
        grid=(MOE_NB,),
        in_specs=[rows, hbm, hbm, hbm],
        out_specs=rows,
        scratch_shapes=[pltpu.VMEM((2, D_MODEL, D_EXPERT), F32), pltpu.VMEM((2, D_MODEL, D_EXPERT), F32),
                        pltpu.VMEM((2, D_EXPERT, D_MODEL), F32),
                        pltpu.VMEM((D_MODEL, D_EXPERT), BF16), pltpu.VMEM((D_MODEL, D_EXPERT), BF16),
                        pltpu.VMEM((D_EXPERT, D_MODEL), BF16), pltpu.SemaphoreType.DMA((2,))],
    )
    return pl.pallas_call(
        functools.partial(_expert_kernel, layer=layer),
        out_shape=jax.ShapeDtypeStruct((MOE_SLOTS * ROW_TILE, LANES), U32),
        grid_spec=grid_spec,
        compiler_params=_cparams(("arbitrary",)),
        name="moe_experts",
    )(*plan, xs, w_gate, w_up, w_down)


def _combine_kernel(dest_ref, rows_ref, gate_ref, sh_ref, x1_ref, g2_ref, lg_ref, lb_ref, o_ref, gbuf, sem):
    tm = x1_ref.shape[0]
    i = pl.program_id(0)
    slot = lax.rem(i, 2)

    def issue(step, s):
        base = step * tm

        def body(t, carry):
            for k in range(TOP_K):
                d = dest_ref[(base + t) * TOP_K + k]
                pltpu.make_async_copy(rows_ref.at[_tile_rows(d)], gbuf.at[s, k, _tile_rows(t)], sem.at[s]).start()
            return carry

        lax.fori_loop(0, tm, body, 0)

    @pl.when(i == 0)
    def _():
        issue(0, 0)

    @pl.when(i + 1 < pl.num_programs(0))
    def _():
        issue(i + 1, 1 - slot)

    for k in range(TOP_K):
        pltpu.make_async_copy(rows_ref.at[pl.ds(0, tm * ROW_TILE)], gbuf.at[slot, k], sem.at[slot]).wait()
    gate = gate_ref[...]
    sh = sh_ref[...]
    lo = sh[:, :HALF]
    hi = sh[:, HALF:]
    for k in range(TOP_K):
        a, b = _unpack_rows(_load_row_tiles(gbuf.at[slot, k], tm))
        gk = gate[:, k:k + 1]
        lo = lo + gk * a
        hi = hi + gk * b
    moe = jnp.concatenate([lo, hi], axis=1)
    o_ref[...] = _layer_norm(ALPHA * x1_ref[...] + g2_ref[...] * moe, lg_ref[...], lb_ref[...])


def _combine_call(dest, out_rows, gate, sh, x1, ada, prm, rows_per_b, tm):
    m = x1.shape[0]
    r = ada.shape[1]
    row = pl.BlockSpec((tm, D_MODEL), lambda i, d: (i, 0))
    vec = pl.BlockSpec((1, D_MODEL), lambda i, d: (0, 0))
    grid_spec = pltpu.PrefetchScalarGridSpec(
        num_scalar_prefetch=1,
        grid=(m // tm,),
        in_specs=[pl.BlockSpec(memory_space=pl.ANY),
                  pl.BlockSpec((tm, LANES), lambda i, d: (i, 0)), row, row,
                  pl.BlockSpec((None, r, D_MODEL), _mod_spec(rows_per_b, tm, 5)), vec, vec],
        out_specs=row,
        scratch_shapes=[pltpu.VMEM((2, TOP_K, tm * ROW_TILE, LANES), U32), pltpu.SemaphoreType.DMA((2,))],
    )
    return pl.pallas_call(
        _combine_kernel,
        out_shape=jax.ShapeDtypeStruct((m, D_MODEL), F32),
        grid_spec=grid_spec,
        compiler_params=_cparams(("arbitrary",)),
        name="moe_combine",
    )(dest, out_rows, gate, sh, x1, ada, prm["ln2_g"], prm["ln2_b"])


def _block_diag(w):
    eye = jnp.eye(LRU_BLOCKS, dtype=w.dtype)
    full = jnp.einsum("hcd,hg->hcgd", w, eye).reshape(W_GROUP, W_GROUP)
    return full.astype(BF16)


def _rope_tables(pos):
    half = HEAD_DIM // 2
    freqs = ROPE_THETA ** (-jnp.arange(half, dtype=F32) / half)
    ang = pos.astype(F32)[:, None] * freqs[None, :]
    cos = jnp.cos(ang)
    sin = jnp.sin(ang)
    cos_t = jnp.concatenate([cos, cos, cos, cos], axis=1)
    sin_t = jnp.concatenate([-sin, sin, -sin, sin], axis=1)
    return cos_t, sin_t


def kernel(x_prompt, x_sample, cache_k, cache_v, state_conv_a, state_lru, state_conv_c, state_conv_d, page_table, c_prompt, c_sample, w_ada, b_ada, w_in, conv_a_w, conv_a_b, lru_wr, lru_br, lru_wi, lru_bi, lru_lambda, lam_q1, lam_k1, lam_q2, lam_k2, subln_g, conv_c_w, conv_c_b, ln_c_g, ln_c_b, conv_d_w, w_out, ln1_g, ln1_b, ln2_g, ln2_b, router_w, router_bias, exp_w_gate, exp_w_up, exp_w_down, sh_w_gate, sh_w_up, sh_w_down):
    n_pool = cache_k.shape[1]
    c_all = jnp.concatenate([c_prompt, c_sample, jnp.zeros((16 - BATCH - DEC_BATCH, D_MODEL), F32)], axis=0)
    ada_all = _ada_call(c_all, w_ada, b_ada)
    w_in_bf = w_in.astype(BF16)
    w_out_bf = w_out.astype(BF16)
    sh_wg_bf = sh_w_gate.astype(BF16)
    sh_wu_bf = sh_w_up.astype(BF16)
    sh_wd_bf = sh_w_down.astype(BF16)
    sh_w = (sh_wg_bf, sh_wu_bf, sh_wd_bf)
    cnt0 = jnp.zeros((1, N_EXPERTS), F32)
    cache_k4 = cache_k.reshape(DEPTH, n_pool, PAGE_SIZE * N_HEADS, LANES)
    cache_v4 = cache_v.reshape(DEPTH, n_pool, PAGE_SIZE * N_HEADS, LANES)
    cos_p, sin_p = _rope_tables(jnp.arange(SEQ, dtype=jnp.int32))
    cos_s, sin_s = _rope_tables(jnp.full((1,), PAST_LEN, jnp.int32))
    lane = jnp.arange(LANES)
    comp_mask = jnp.stack([lane < HEAD_DIM, lane >= HEAD_DIM]).astype(F32)

    xp = x_prompt.reshape(T_PROMPT, D_MODEL)
    xs = x_sample.reshape(DEC_BATCH, D_MODEL)
    outs = [[] for _ in range(12)]
    for l in range(DEPTH):
        lam_init = 0.8 - 0.6 * math.exp(-0.3 * l)
        prm = dict(
            conv_a_w=conv_a_w[l], conv_a_b=conv_a_b[l][None], wr=_block_diag(lru_wr[l]), br=lru_br[l][None],
            wi=_block_diag(lru_wi[l]), bi=lru_bi[l][None], lam=lru_lambda[l][None],
            conv_c_w=conv_c_w[l], conv_c_b=conv_c_b[l][None], ln_c_g=ln_c_g[l][None], ln_c_b=ln_c_b[l][None],
            conv_d_w=conv_d_w[l],
            lam_q1=lam_q1[l][None], lam_k1=lam_k1[l][None], lam_q2=lam_q2[l][None], lam_k2=lam_k2[l][None],
            subln_g=subln_g[l][None], subln_g_col=subln_g[l][:, None], lam_init=jnp.full((1, 1), lam_init, F32),
            ln1_g=ln1_g[l][None], ln1_b=ln1_b[l][None], ln2_g=ln2_g[l][None], ln2_b=ln2_b[l][None],
            router_w_bf=router_w[l].astype(BF16), router_bias=router_bias[l][None],
        )
        ada_p = ada_all[l, :BATCH].reshape(BATCH, 1, 6 * D_MODEL)
        ada_s = ada_all[l, BATCH:BATCH + DEC_BATCH].reshape(1, DEC_BATCH, 6 * D_MODEL)

        proj_p = _inproj_call(xp, ada_p, w_in_bf, l, SEQ, 1024, 512).reshape(BATCH, SEQ, D_IN)
        ya, yc, yd, q_p, k_p, sa, sh, sc, sd = _mixer_prompt_call(proj_p, cos_p, sin_p, prm)
        yb = _flash_call(q_p, k_p, proj_p, prm)
        outs[0].append(k_p)
        outs[1].append(proj_p[:, :, 4 * W_GROUP:5 * W_GROUP])
        outs[2].append(sa)
        outs[3].append(sh.reshape(BATCH, W_GROUP))
        outs[4].append(sc)
        outs[5].append(sd)

        proj_s = _inproj_call(xs, ada_s, w_in_bf, l, DEC_BATCH, DEC_BATCH, 512)
        ya_s, yc_s, yd_s, q_s, k_s, sa_s, sh_s, sc_s, sd_s = _mixer_sample_call(
            proj_s, cos_s, sin_s, prm,
            state_conv_a[l].transpose(1, 0, 2), state_lru[l], state_conv_c[l].transpose(1, 0, 2),
            state_conv_d[l].transpose(1, 0, 2))
        v_s = proj_s[:, 4 * W_GROUP:5 * W_GROUP]
        q4 = q_s.reshape(DEC_BATCH, 1, N_HEADS, LANES) * (HEAD_DIM ** -0.5)
        q8 = (q4 * comp_mask[None, :, None, :]).reshape(DEC_BATCH, 2 * N_HEADS, LANES).astype(BF16)
        pad4 = jnp.zeros((DEC_BATCH, N_HEADS, LANES), F32)
        k_new8 = jnp.concatenate([k_s.reshape(DEC_BATCH, N_HEADS, LANES), pad4], axis=1)
        v_new8 = jnp.concatenate([v_s.reshape(DEC_BATCH, N_HEADS, LANES), pad4], axis=1)
        yb_s = _decode_call(q8, k_new8, v_new8, cache_k4, cache_v4, page_table, prm, l)
        yb_s = yb_s.reshape(DEC_BATCH, W_GROUP).astype(BF16)
        outs[6].append(k_s.reshape(DEC_BATCH, 1, N_HEADS, 2 * HEAD_DIM))
        outs[7].append(v_s.reshape(DEC_BATCH, 1, N_HEADS, 2 * HEAD_DIM))
        outs[8].append(sa_s.transpose(1, 0, 2))
        outs[9].append(sh_s)
        outs[10].append(sc_s.transpose(1, 0, 2))
        outs[11].append(sd_s.transpose(1, 0, 2))

        flat = lambda t: t.reshape(T_PROMPT, W_GROUP)
        x1_p, shared_p, gate_p, dest_p, cnt_p, x_sorted = _outproj_call(
            flat(ya), flat(yb), flat(yc), flat(yd), w_out_bf, xp, ada_p, prm, sh_w, cnt0, l, SEQ, 256)
        x1_s, shared_s, gate_s, dest_s, cnt_all, x_sorted = _outproj_call(
            ya_s, yb_s, yc_s, yd_s, w_out_bf, xs, ada_s, prm, sh_w, cnt_p, l, DEC_BATCH, DEC_BATCH, xs_prev=x_sorted)

        plan = _expert_plan(cnt_all[0].astype(jnp.int32))
        out_rows = _expert_call(plan, x_sorted, exp_w_gate, exp_w_up, exp_w_down, l)
        xp = _combine_call(dest_p[:, :TOP_K].reshape(-1), out_rows, gate_p, shared_p, x1_p, ada_p, prm, SEQ, 128)
        xs = _combine_call(dest_s[:, :TOP_K].reshape(-1), out_rows, gate_s, shared_s, x1_s, ada_s, prm, DEC_BATCH, DEC_BATCH)

    st = [jnp.stack(o, axis=0) for o in outs]
    kv_shape = (DEPTH, BATCH, SEQ // PAGE_SIZE, PAGE_SIZE, N_HEADS, 2 * HEAD_DIM)
    return (xp.reshape(BATCH, SEQ, D_MODEL), xs.reshape(DEC_BATCH, 1, D_MODEL),
            st[0].reshape(kv_shape), st[1].reshape(kv_shape), st[2], st[3], st[4], st[5],
            st[6], st[7], st[8], st[9], st[10], st[11])
```

```python
import functools
import math

import jax
import jax.numpy as jnp
from jax import lax
from jax.experimental import pallas as pl
from jax.experimental.pallas import tpu as pltpu

F32 = jnp.float32
BF16 = jnp.bfloat16

D_MODEL = 2048
BATCH = 2
SEQ = 4096
DEPTH = 4
DEC_BATCH = 8
PAST_LEN = 16384
PAGE_SIZE = 128
W_GROUP = 512
N_IN_PARTS = 10
D_IN = N_IN_PARTS * W_GROUP
CONV_A = 4
LRU_BLOCKS = 8
LRU_BLOCK_DIM = W_GROUP // LRU_BLOCKS
LRU_C = 8.0
HEAD_DIM = 64
N_HEADS = 4
ROPE_THETA = 10000.0
CONV_C = 31
CONV_D = 3
N_EXPERTS = 64
TOP_K = 8
D_EXPERT = 512
ROUTE_SCALE = 2.5
LN_EPS = 1e-5
ALPHA = (2.0 * DEPTH) ** 0.25
NEG_INF = -1e30

LANES = 128
VMEM_LIMIT = 56 * 1024 * 1024

T_PROMPT = BATCH * SEQ
T_ALL = T_PROMPT + DEC_BATCH
MOE_BM = 256
MOE_NB = -(-T_ALL * TOP_K // MOE_BM) + N_EXPERTS
N_PAGES = PAST_LEN // PAGE_SIZE
DEC_G = 16


def _cparams(sem, **kw):
    return pltpu.CompilerParams(dimension_semantics=sem, vmem_limit_bytes=VMEM_LIMIT, **kw)


def _dot(a, b):
    return jnp.dot(a, b, preferred_element_type=F32)


def _dot_nt(a, b):
    return lax.dot_general(a, b, (((1,), (1,)), ((), ())), preferred_element_type=F32)


def _sigmoid(x):
    return 1.0 / (1.0 + jnp.exp(-x))


def _silu(x):
    return x * _sigmoid(x)


def _layer_norm(z, g, b):
    mu = jnp.mean(z, axis=-1, keepdims=True)
    zc = z - mu
    var = jnp.mean(zc * zc, axis=-1, keepdims=True)
    return zc * lax.rsqrt(var + LN_EPS) * g + b


def _ada_kernel(c_ref, w_ref, b_ref, o_ref):
    c = c_ref[...]
    o_ref[...] = _dot(_silu(c).astype(BF16), w_ref[...].astype(BF16)) + b_ref[...]


def _ada_call(c_all, w_ada, b_ada):
    r = c_all.shape[0]
    tn = 1024
    return pl.pallas_call(
        _ada_kernel,
        out_shape=jax.ShapeDtypeStruct((DEPTH, r, 6 * D_MODEL), F32),
        grid=(DEPTH, 6 * D_MODEL // tn),
        in_specs=[
            pl.BlockSpec((r, D_MODEL), lambda l, n: (0, 0)),
            pl.BlockSpec((None, D_MODEL, tn), lambda l, n: (l, 0, n)),
            pl.BlockSpec((None, 1, tn), lambda l, n: (l, 0, n)),
        ],
        out_specs=pl.BlockSpec((None, r, tn), lambda l, n: (l, 0, n)),
        compiler_params=_cparams(("arbitrary", "arbitrary")),
        name="ada",
    )(c_all, w_ada, b_ada.reshape(DEPTH, 1, 6 * D_MODEL))


def _inproj_kernel(x_ref, sc_ref, sh_ref, w_ref, o_ref, h_scr):
    @pl.when(pl.program_id(1) == 0)
    def _():
        h_scr[...] = (x_ref[...] * (1.0 + sc_ref[...]) + sh_ref[...]).astype(BF16)

    o_ref[...] = _dot(h_scr[...], w_ref[...])


def _mod_spec(rows_per_b, tm, col):
    def imap(m, *_):
        return ((m * tm) // rows_per_b, 0, col)
    return imap


def _inproj_call(x, ada, w_in_bf, layer, rows_per_b, tm, tn):
    m = x.shape[0]
    r = ada.shape[1]
    return pl.pallas_call(
        _inproj_kernel,
        out_shape=jax.ShapeDtypeStruct((m, D_IN), F32),
        grid=(m // tm, D_IN // tn),
        in_specs=[
            pl.BlockSpec((tm, D_MODEL), lambda i, n: (i, 0)),
            pl.BlockSpec((None, r, D_MODEL), _mod_spec(rows_per_b, tm, 1)),
            pl.BlockSpec((None, r, D_MODEL), _mod_spec(rows_per_b, tm, 0)),
            pl.BlockSpec((None, D_MODEL, tn), lambda i, n: (layer, 0, n)),
        ],
        out_specs=pl.BlockSpec((tm, tn), lambda i, n: (i, n)),
        scratch_shapes=[pltpu.VMEM((tm, D_MODEL), BF16)],
        compiler_params=_cparams(("arbitrary", "arbitrary")),
        name="inproj",
    )(x, ada, ada, w_in_bf)


def _softplus(z):
    return jnp.maximum(z, 0.0) + jnp.log1p(jnp.exp(-jnp.abs(z)))


def _gelu_tanh(x):
    return 0.5 * x * (1.0 + jnp.tanh(math.sqrt(2.0 / math.pi) * (x + 0.044715 * (x * x * x))))


def _lru_gates(xa, wr, br, wi, bi, lam):
    xb = xa.astype(BF16)
    r = _sigmoid(_dot(xb, wr) + br)
    i = _sigmoid(_dot(xb, wi) + bi)
    log_a = -LRU_C * r * _softplus(-lam)
    a = jnp.exp(log_a)
    u = jnp.sqrt(jnp.tanh(-log_a) * (1.0 + a * a)) * (i * xa)
    return a, u


def _rope(x, cos_t, sin_t):
    lane = lax.broadcasted_iota(jnp.int32, x.shape, 1)
    first_half = (lane & (HEAD_DIM // 2)) == 0
    w = x.shape[1]
    partner = jnp.where(first_half, pltpu.roll(x, w - HEAD_DIM // 2, axis=1), pltpu.roll(x, HEAD_DIM // 2, axis=1))
    return x * cos_t + partner * sin_t


def _tile4(t):
    return jnp.concatenate([t, t, t, t], axis=1)


def _conv_from_buf(buf_ref, w_ref, hist, n_taps, rows, rb):
    outs = []
    for r0 in range(0, rows, rb):
        acc = None
        for j in range(n_taps):
            term = buf_ref[pl.ds(hist - (n_taps - 1) + j + r0, rb), :] * w_ref[pl.ds(j, 1), :]
            acc = term if acc is None else acc + term
        outs.append(acc)
    return jnp.concatenate(outs, axis=0) if len(outs) > 1 else outs[0]


def _linear_scan(a, b):
    n = a.shape[0]
    rows = lax.broadcasted_iota(jnp.int32, a.shape, 0)
    d = 1
    while d < n:
        keep = rows >= d
        a_sh = jnp.where(keep, pltpu.roll(a, d, axis=0), 1.0)
        b_sh = jnp.where(keep, pltpu.roll(b, d, axis=0), 0.0)
        b = a * b_sh + b
        a = a * a_sh
        d *= 2
    return a, b


MIX_TC = 256
HIST_A = 8
HIST_C = 32
HIST_D = 8
CONV_RB = 32


def _mixer_prompt_kernel(ax_ref, ag_ref, bq_ref, bk_ref, cv_ref, cg_ref, db_ref, dc_ref, dx_ref,
                         cos_ref, sin_ref, caw_ref, cab_ref, wr_ref, br_ref, wi_ref, bi_ref, lam_ref,
                         ccw_ref, ccb_ref, lng_ref, lnb_ref, cdw_ref,
                         ya_ref, yc_ref, yd_ref, q_ref, k_ref, sa_ref, sh_ref, sc_ref, sd_ref,
                         xa_buf, uc_buf, dd_buf, h_buf):
    tc = MIX_TC

    @pl.when(pl.program_id(1) == 0)
    def _():
        xa_buf[pl.ds(0, HIST_A), :] = jnp.zeros((HIST_A, W_GROUP), F32)
        uc_buf[pl.ds(0, HIST_C), :] = jnp.zeros((HIST_C, W_GROUP), F32)
        dd_buf[pl.ds(0, HIST_D), :] = jnp.zeros((HIST_D, W_GROUP), F32)
        h_buf[...] = jnp.zeros((1, W_GROUP), F32)

    xa_buf[pl.ds(HIST_A, tc), :] = ax_ref[...]
    xa = _conv_from_buf(xa_buf, caw_ref, HIST_A, CONV_A, tc, CONV_RB) + cab_ref[...]
    a, u = _lru_gates(xa, wr_ref[...], br_ref[...], wi_ref[...], bi_ref[...], lam_ref[...])
    a_cum, b_cum = _linear_scan(a, u)
    h = a_cum * h_buf[...] + b_cum
    ya_ref[...] = (_gelu_tanh(ag_ref[...]) * h).astype(BF16)
    h_last = h[tc - 1:tc, :]
    h_buf[...] = h_last
    sh_ref[...] = h_last
    sa_ref[...] = xa_buf[pl.ds(HIST_A + tc - (CONV_A - 1), CONV_A - 1), :]
    xa_buf[pl.ds(0, HIST_A), :] = xa_buf[pl.ds(tc, HIST_A), :]

    uc_buf[pl.ds(HIST_C, tc), :] = cv_ref[...] * _sigmoid(cg_ref[...])
    yc = _conv_from_buf(uc_buf, ccw_ref, HIST_C, CONV_C, tc, CONV_RB) + ccb_ref[...]
    yc_ref[...] = _silu(_layer_norm(yc, lng_ref[...], lnb_ref[...])).astype(BF16)
    sc_ref[...] = uc_buf[pl.ds(HIST_C + tc - (CONV_C - 1), CONV_C - 1), :]
    uc_buf[pl.ds(0, HIST_C), :] = uc_buf[pl.ds(tc, HIST_C), :]

    dd_buf[pl.ds(HIST_D, tc), :] = dc_ref[...] * dx_ref[...]
    yd = _conv_from_buf(dd_buf, cdw_ref, HIST_D, CONV_D, tc, CONV_RB)
    yd_ref[...] = (db_ref[...] * yd).astype(BF16)
    sd_ref[...] = dd_buf[pl.ds(HIST_D + tc - (CONV_D - 1), CONV_D - 1), :]
    dd_buf[pl.ds(0, HIST_D), :] = dd_buf[pl.ds(tc, HIST_D), :]

    cos_t = _tile4(cos_ref[...])
    sin_t = _tile4(sin_ref[...])
    q_ref[...] = (_rope(bq_ref[...], cos_t, sin_t) * (HEAD_DIM ** -0.5)).astype(BF16)
    k_ref[...] = _rope(bk_ref[...], cos_t, sin_t)


def _mixer_prompt_call(proj, cos_t, sin_t, prm):
    tc = MIX_TC

    def col(j):
        return pl.BlockSpec((None, tc, W_GROUP), lambda b, c: (b, c, j))

    def full(arr):
        nd = arr.ndim
        return pl.BlockSpec(arr.shape, lambda b, c: (0,) * nd)

    small = [prm["conv_a_w"], prm["conv_a_b"], prm["wr"], prm["br"], prm["wi"], prm["bi"], prm["lam"],
             prm["conv_c_w"], prm["conv_c_b"], prm["ln_c_g"], prm["ln_c_b"], prm["conv_d_w"]]
    tok = pl.BlockSpec((None, tc, W_GROUP), lambda b, c: (b, c, 0))

    def state(n):
        return pl.BlockSpec((None, n, W_GROUP), lambda b, c: (b, 0, 0))

    out_shape = (
        jax.ShapeDtypeStruct((BATCH, SEQ, W_GROUP), BF16),
        jax.ShapeDtypeStruct((BATCH, SEQ, W_GROUP), BF16),
        jax.ShapeDtypeStruct((BATCH, SEQ, W_GROUP), BF16),
        jax.ShapeDtypeStruct((BATCH, SEQ, W_GROUP), BF16),
        jax.ShapeDtypeStruct((BATCH, SEQ, W_GROUP), F32),
        jax.ShapeDtypeStruct((BATCH, CONV_A - 1, W_GROUP), F32),
        jax.ShapeDtypeStruct((BATCH, 1, W_GROUP), F32),
        jax.ShapeDtypeStruct((BATCH, CONV_C - 1, W_GROUP), F32),
        jax.ShapeDtypeStruct((BATCH, CONV_D - 1, W_GROUP), F32),
    )
    return pl.pallas_call(
        _mixer_prompt_kernel,
        out_shape=out_shape,
        grid=(BATCH, SEQ // tc),
        in_specs=[col(0), col(1), col(2), col(3), col(5), col(6), col(7), col(8), col(9),
                  pl.BlockSpec((tc, LANES), lambda b, c: (c, 0)),
                  pl.BlockSpec((tc, LANES), lambda b, c: (c, 0))] + [full(a) for a in small],
        out_specs=(tok, tok, tok, tok, tok, state(CONV_A - 1), state(1), state(CONV_C - 1), state(CONV_D - 1)),
        scratch_shapes=[pltpu.VMEM((HIST_A + tc, W_GROUP), F32), pltpu.VMEM((HIST_C + tc, W_GROUP), F32),
                        pltpu.VMEM((HIST_D + tc, W_GROUP), F32), pltpu.VMEM((1, W_GROUP), F32)],
        compiler_params=_cparams(("arbitrary", "arbitrary")),
        name="mixer_prompt",
    )(*([proj] * 9), cos_t, sin_t, *small)


def _mixer_sample_kernel(ax_ref, ag_ref, bq_ref, bk_ref, cv_ref, cg_ref, db_ref, dc_ref, dx_ref,
                         cos_ref, sin_ref, caw_ref, cab_ref, wr_ref, br_ref, wi_ref, bi_ref, lam_ref,
                         ccw_ref, ccb_ref, lng_ref, lnb_ref, cdw_ref,
                         ba_ref, h0_ref, bc_ref, bd_ref,
                         ya_ref, yc_ref, yd_ref, q_ref, k_ref, sa_ref, sh_ref, sc_ref, sd_ref):
    ax = ax_ref[...]
    xa = cab_ref[...] + caw_ref[pl.ds(CONV_A - 1, 1), :] * ax
    for j in range(CONV_A - 1):
        xa = xa + caw_ref[pl.ds(j, 1), :] * ba_ref[j]
    a, u = _lru_gates(xa, wr_ref[...], br_ref[...], wi_ref[...], bi_ref[...], lam_ref[...])
    h = a * h0_ref[...] + u
    ya_ref[...] = (_gelu_tanh(ag_ref[...]) * h).astype(BF16)
    sh_ref[...] = h
    for j in range(CONV_A - 2):
        sa_ref[j] = ba_ref[j + 1]
    sa_ref[CONV_A - 2] = ax

    uc = cv_ref[...] * _sigmoid(cg_ref[...])
    yc = ccb_ref[...] + ccw_ref[pl.ds(CONV_C - 1, 1), :] * uc
    for j in range(CONV_C - 1):
        yc = yc + ccw_ref[pl.ds(j, 1), :] * bc_ref[j]
    yc_ref[...] = _silu(_layer_norm(yc, lng_ref[...], lnb_ref[...])).astype(BF16)
    for j in range(CONV_C - 2):
        sc_ref[j] = bc_ref[j + 1]
    sc_ref[CONV_C - 2] = uc

    dd = dc_ref[...] * dx_ref[...]
    yd = cdw_ref[pl.ds(CONV_D - 1, 1), :] * dd
    for j in range(CONV_D - 1):
        yd = yd + cdw_ref[pl.ds(j, 1), :] * bd_ref[j]
    yd_ref[...] = (db_ref[...] * yd).astype(BF16)
    for j in range(CONV_D - 2):
        sd_ref[j] = bd_ref[j + 1]
    sd_ref[CONV_D - 2] = dd

    cos_t = _tile4(cos_ref[...])
    sin_t = _tile4(sin_ref[...])
    q_ref[...] = _rope(bq_ref[...], cos_t, sin_t)
    k_ref[...] = _rope(bk_ref[...], cos_t, sin_t)


def _mixer_sample_call(proj, cos_t, sin_t, prm, buf_a, h0, buf_c, buf_d):
    nb = DEC_BATCH

    def col(j):
        return pl.BlockSpec((nb, W_GROUP), lambda i: (0, j))

    def full(arr):
        nd = arr.ndim
        return pl.BlockSpec(arr.shape, lambda i: (0,) * nd)

    small = [prm["conv_a_w"], prm["conv_a_b"], prm["wr"], prm["br"], prm["wi"], prm["bi"], prm["lam"],
             prm["conv_c_w"], prm["conv_c_b"], prm["ln_c_g"], prm["ln_c_b"], prm["conv_d_w"]]
    states = [buf_a, h0, buf_c, buf_d]
    tok_bf = jax.ShapeDtypeStruct((nb, W_GROUP), BF16)
    tok_f = jax.ShapeDtypeStruct((nb, W_GROUP), F32)
    out_shape = (tok_bf, tok_bf, tok_bf, tok_f, tok_f,
                 jax.ShapeDtypeStruct(buf_a.shape, F32), tok_f,
                 jax.ShapeDtypeStruct(buf_c.shape, F32), jax.ShapeDtypeStruct(buf_d.shape, F32))
    return pl.pallas_call(
        _mixer_sample_kernel,
        out_shape=out_shape,
        grid=(1,),
        in_specs=[col(0), col(1), col(2), col(3), col(5), col(6), col(7), col(8), col(9),
                  full(cos_t), full(sin_t)] + [full(a) for a in small] + [full(a) for a in states],
        out_specs=tuple(pl.BlockSpec(s.shape, lambda i, nd=len(s.shape): (0,) * nd) for s in out_shape),
        compiler_params=_cparams(("arbitrary",)),
        name="mixer_sample",
    )(*([proj] * 9), cos_t, sin_t, *small, *states)


def _diff_lambda(lq1_ref, lk1_ref, lq2_ref, lk2_ref, lam_init):
    s1 = jnp.sum(lq1_ref[...] * lk1_ref[...], axis=-1, keepdims=True)
    s2 = jnp.sum(lq2_ref[...] * lk2_ref[...], axis=-1, keepdims=True)
    return jnp.exp(s1) - jnp.exp(s2) + lam_init


def _diff_finish(o1, o2, lam, g, lam_init):
    od = o1 - lam * o2
    ms = jnp.mean(od * od, axis=-1, keepdims=True)
    return od * lax.rsqrt(ms + LN_EPS) * g * (1.0 - lam_init)


FLASH_T = 512


def _flash_kernel(q_ref, k_ref, v_ref, lq1_ref, lk1_ref, lq2_ref, lk2_ref, gcol_ref, li_ref, o_ref,
                  kb_scr, vt_scr, m_scr, l_scr, acc_scr):
    t = FLASH_T
    i = pl.program_id(2)

    @pl.when(i == 0)
    def _():
        for c in range(SEQ // t):
            kb_scr[c] = k_ref[pl.ds(c * t, t), :].astype(BF16)
            vt_scr[c] = v_ref[pl.ds(c * t, t), :].T.astype(BF16)

    q = q_ref[...]
    lane = lax.broadcasted_iota(jnp.int32, q.shape, 1)
    zero = jnp.zeros_like(q)
    qq = jnp.concatenate([jnp.where(lane < HEAD_DIM, q, zero), jnp.where(lane >= HEAD_DIM, q, zero)], axis=0)
    m_scr[...] = jnp.full(m_scr.shape, NEG_INF, F32)
    l_scr[...] = jnp.zeros(l_scr.shape, F32)
    acc_scr[...] = jnp.zeros(acc_scr.shape, F32)

    def step(j, masked):
        st = _dot_nt(kb_scr[j], qq)
        if masked:
            r = lax.broadcasted_iota(jnp.int32, st.shape, 0)
            c = lax.broadcasted_iota(jnp.int32, st.shape, 1)
            st = jnp.where(r <= jnp.where(c >= t, c - t, c), st, NEG_INF)
        m_prev = m_scr[...]
        m_new = jnp.maximum(m_prev, jnp.max(st, axis=0, keepdims=True))
        alpha = jnp.exp(m_prev - m_new)
        p = jnp.exp(st - m_new)
        l_scr[...] = alpha * l_scr[...] + jnp.sum(p, axis=0, keepdims=True)
        acc_scr[...] = alpha * acc_scr[...] + _dot(vt_scr[j], p.astype(BF16))
        m_scr[...] = m_new

    def body(j, carry):
        step(j, False)
        return carry

    lax.fori_loop(0, i, body, 0)
    step(i, True)

    o = acc_scr[...] / l_scr[...]
    lam_init = li_ref[...]
    lam = _diff_lambda(lq1_ref, lk1_ref, lq2_ref, lk2_ref, lam_init)
    od = o[:, :t] - lam * o[:, t:]
    ms = jnp.mean(od * od, axis=0, keepdims=True)
    y = od * lax.rsqrt(ms + LN_EPS) * gcol_ref[...] * (1.0 - lam_init)
    o_ref[...] = y.T.astype(BF16)


def _flash_call(q, k, proj, prm):
    t = FLASH_T
    v_col0 = 4 * W_GROUP // LANES

    def full(arr):
        nd = arr.ndim
        return pl.BlockSpec(arr.shape, lambda b, h, i: (0,) * nd)

    small = [prm["lam_q1"], prm["lam_k1"], prm["lam_q2"], prm["lam_k2"], prm["subln_g_col"], prm["lam_init"]]
    return pl.pallas_call(
        _flash_kernel,
        out_shape=jax.ShapeDtypeStruct((BATCH, SEQ, W_GROUP), BF16),
        grid=(BATCH, N_HEADS, SEQ // t),
        in_specs=[pl.BlockSpec((None, t, LANES), lambda b, h, i: (b, i, h)),
                  pl.BlockSpec((None, SEQ, LANES), lambda b, h, i: (b, 0, h)),
                  pl.BlockSpec((None, SEQ, LANES), lambda b, h, i: (b, 0, v_col0 + h))] + [full(a) for a in small],
        out_specs=pl.BlockSpec((None, t, LANES), lambda b, h, i: (b, i, h)),
        scratch_shapes=[pltpu.VMEM((SEQ // t, t, LANES), BF16), pltpu.VMEM((SEQ // t, LANES, t), BF16),
                        pltpu.VMEM((1, 2 * t), F32), pltpu.VMEM((1, 2 * t), F32), pltpu.VMEM((LANES, 2 * t), F32)],
        compiler_params=_cparams(("arbitrary", "arbitrary", "arbitrary")),
        name="flash",
    )(q, k, proj, *small)


def _softmax_update(s, v, m_scr, l_scr, acc_scr):
    m_prev = m_scr[...]
    m_new = jnp.maximum(m_prev, jnp.max(s, axis=1, keepdims=True))
    alpha = jnp.exp(m_prev - m_new)
    p = jnp.exp(s - m_new)
    l_scr[...] = alpha * l_scr[...] + jnp.sum(p, axis=1, keepdims=True)
    acc_scr[...] = alpha * acc_scr[...] + _dot(p.astype(BF16), v)
    m_scr[...] = m_new


def _decode_kernel(pt_ref, q_ref, *refs):
    k_refs = refs[:DEC_G]
    v_refs = refs[DEC_G:2 * DEC_G]
    (kn_ref, vn_ref, lq1_ref, lk1_ref, lq2_ref, lk2_ref, g_ref, li_ref, o_ref, m_scr, l_scr, acc_scr) = refs[2 * DEC_G:]
    s_idx = pl.program_id(1)

    @pl.when(s_idx == 0)
    def _():
        m_scr[...] = jnp.full(m_scr.shape, NEG_INF, F32)
        l_scr[...] = jnp.zeros(l_scr.shape, F32)
        acc_scr[...] = jnp.zeros(acc_scr.shape, F32)

    q = q_ref[...]
    n_kv = PAGE_SIZE * N_HEADS
    row = lax.broadcasted_iota(jnp.int32, (2 * N_HEADS, n_kv), 0)
    col = lax.broadcasted_iota(jnp.int32, (2 * N_HEADS, n_kv), 1)
    same_head = (col & (N_HEADS - 1)) == (row & (N_HEADS - 1))
    s = jnp.concatenate([jnp.where(same_head, _dot_nt(q, k_refs[g][...].astype(BF16)), NEG_INF)
                         for g in range(DEC_G)], axis=1)
    m_prev = m_scr[...]
    m_new = jnp.maximum(m_prev, jnp.max(s, axis=1, keepdims=True))
    alpha = jnp.exp(m_prev - m_new)
    p = jnp.exp(s - m_new)
    l_scr[...] = alpha * l_scr[...] + jnp.sum(p, axis=1, keepdims=True)
    pb = p.astype(BF16)
    pv = _dot(pb[:, :n_kv], v_refs[0][...].astype(BF16))
    for g in range(1, DEC_G):
        pv = pv + _dot(pb[:, g * n_kv:(g + 1) * n_kv], v_refs[g][...].astype(BF16))
    acc_scr[...] = alpha * acc_scr[...] + pv
    m_scr[...] = m_new

    @pl.when(s_idx == pl.num_programs(1) - 1)
    def _():
        kn = kn_ref[...].astype(BF16)
        vn = vn_ref[...].astype(BF16)
        r8 = lax.broadcasted_iota(jnp.int32, (2 * N_HEADS, 2 * N_HEADS), 0)
        c8 = lax.broadcasted_iota(jnp.int32, (2 * N_HEADS, 2 * N_HEADS), 1)
        s = jnp.where(c8 == (r8 & (N_HEADS - 1)), _dot_nt(q, kn), NEG_INF)
        _softmax_update(s, vn, m_scr, l_scr, acc_scr)
        o = acc_scr[...] / l_scr[...]
        lam_init = li_ref[...]
        lam = _diff_lambda(lq1_ref, lk1_ref, lq2_ref, lk2_ref, lam_init)
        o_ref[...] = _diff_finish(o[:N_HEADS], o[N_HEADS:], lam, g_ref[...], lam_init)


def _decode_call(q8, k_new8, v_new8, cache_k4, cache_v4, page_table, prm, layer):
    n_kv = PAGE_SIZE * N_HEADS
    steps = N_PAGES // DEC_G

    def page_spec(g):
        return pl.BlockSpec((None, None, n_kv, LANES), lambda b, s, pt: (layer, pt[b, s * DEC_G + g], 0, 0))

    def full(arr):
        nd = arr.ndim
        return pl.BlockSpec(arr.shape, lambda b, s, pt: (0,) * nd)

    row8 = pl.BlockSpec((None, 2 * N_HEADS, LANES), lambda b, s, pt: (b, 0, 0))
    small = [prm["lam_q1"], prm["lam_k1"], prm["lam_q2"], prm["lam_k2"], prm["subln_g"], prm["lam_init"]]
    grid_spec = pltpu.PrefetchScalarGridSpec(
        num_scalar_prefetch=1,
        grid=(DEC_BATCH, steps),
        in_specs=[row8] + [page_spec(g) for g in range(DEC_G)] + [page_spec(g) for g in range(DEC_G)]
                 + [row8, row8] + [full(a) for a in small],
        out_specs=pl.BlockSpec((None, N_HEADS, LANES), lambda b, s, pt: (b, 0, 0)),
        scratch_shapes=[pltpu.VMEM((2 * N_HEADS, 1), F32), pltpu.VMEM((2 * N_HEADS, 1), F32),
                        pltpu.VMEM((2 * N_HEADS, LANES), F32)],
    )
    return pl.pallas_call(
        _decode_kernel,
        out_shape=jax.ShapeDtypeStruct((DEC_BATCH, N_HEADS, LANES), F32),
        grid_spec=grid_spec,
        compiler_params=_cparams(("arbitrary", "arbitrary")),
        name="decode_attn",
    )(page_table, q8, *([cache_k4] * DEC_G), *([cache_v4] * DEC_G), k_new8, v_new8, *small)


MOE_CAP = -(-T_ALL // MOE_BM) * MOE_BM
MOE_CAP_BLKS = MOE_CAP // MOE_BM
MOE_SLOTS = N_EXPERTS * MOE_CAP
HALF = D_MODEL // 2
U32 = jnp.uint32


def _pack_rows(x):
    a = lax.bitcast_convert_type(x[:, :HALF].astype(BF16).astype(F32), U32)
    b = lax.bitcast_convert_type(x[:, HALF:].astype(BF16).astype(F32), U32)
    return a | (b >> 16)


def _unpack_rows(w):
    a = lax.bitcast_convert_type(w & U32(0xFFFF0000), F32)
    b = lax.bitcast_convert_type(w << 16, F32)
    return a, b


ROW_TILE = HALF // LANES


def _store_row_tiles(ref, packed):
    n = packed.shape[0]
    for s in range(ROW_TILE):
        ref[pl.ds(s, n, stride=ROW_TILE), :] = packed[:, s * LANES:(s + 1) * LANES]


def _load_row_tiles(ref, n):
    return jnp.concatenate([ref[pl.ds(s, n, stride=ROW_TILE), :] for s in range(ROW_TILE)], axis=1)


def _tile_rows(i):
    return pl.ds(pl.multiple_of(i * ROW_TILE, ROW_TILE), ROW_TILE)


def _shared_ffn(h, wg, wu, wd):
    hb = h.astype(BF16)
    return _dot((_silu(_dot(hb, wg)) * _dot(hb, wu)).astype(BF16), wd)


def _outproj_kernel(ya_ref, yb_ref, yc_ref, yd_ref, w_ref, x_ref, g1_ref, sc2_ref, sh2_ref, lg_ref, lb_ref,
                    rw_ref, rb_ref, swg_ref, swu_ref, swd_ref, cnt_in_ref, *refs, aliased):
    if aliased:
        refs = refs[1:]
    x1_ref, sh_ref, gate_ref, dest_ref, cnt_ref, xs_ref, xpk, dest_v, dest_s, cnt_scr, sem_rows, sem_d = refs
    i = pl.program_id(0)
    last = pl.num_programs(0) - 1
    tm = x_ref.shape[0]

    @pl.when(i == 0)
    def _():
        cnt_scr[...] = cnt_in_ref[...]

    mix = (_dot(ya_ref[...], w_ref[pl.ds(0, W_GROUP), :]) + _dot(yb_ref[...], w_ref[pl.ds(W_GROUP, W_GROUP), :])
           + _dot(yc_ref[...], w_ref[pl.ds(2 * W_GROUP, W_GROUP), :]) + _dot(yd_ref[...], w_ref[pl.ds(3 * W_GROUP, W_GROUP), :]))
    x1 = _layer_norm(ALPHA * x_ref[...] + g1_ref[...] * mix, lg_ref[...], lb_ref[...])
    x1_ref[...] = x1
    h2 = x1 * (1.0 + sc2_ref[...]) + sh2_ref[...]

    sh_ref[...] = _shared_ffn(h2, swg_ref[...], swu_ref[...], swd_ref[...])

    scores = _sigmoid(_dot(h2.astype(BF16), rw_ref[...]))
    sel = scores + rb_ref[...]
    lane_e = lax.broadcasted_iota(jnp.int32, (tm, N_EXPERTS), 1).astype(F32)
    lane_o = lax.broadcasted_iota(jnp.int32, (tm, LANES), 1)
    gate_acc = jnp.zeros((tm, LANES), F32)
    gsum = jnp.zeros((tm, 1), F32)
    chosen = jnp.zeros((tm, N_EXPERTS), F32)
    picks = []
    for k in range(TOP_K):
        best = jnp.max(sel, axis=1, keepdims=True)
        ik = jnp.min(jnp.where(sel == best, lane_e, float(N_EXPERTS)), axis=1, keepdims=True)
        hit = lane_e == ik
        gk = jnp.sum(jnp.where(hit, scores, 0.0), axis=1, keepdims=True)
        sel = jnp.where(hit, -jnp.inf, sel)
        gate_acc = jnp.where(lane_o == k, gk, gate_acc)
        gsum = gsum + gk
        chosen = jnp.where(hit, 1.0, chosen)
        picks.append((ik, hit))
    gate_ref[...] = gate_acc / gsum * ROUTE_SCALE

    r = lax.broadcasted_iota(jnp.int32, (tm, tm), 0)
    c = lax.broadcasted_iota(jnp.int32, (tm, tm), 1)
    earlier = jnp.where(c < r, 1.0, 0.0).astype(BF16)
    rank = cnt_scr[...] + _dot(earlier, chosen.astype(BF16))
    cnt_new = cnt_scr[...] + jnp.sum(chosen, axis=0, keepdims=True)
    cnt_scr[...] = cnt_new
    cnt_ref[...] = cnt_new
    dest_acc = jnp.zeros((tm, LANES), F32)
    for k, (ik, hit) in enumerate(picks):
        pos = jnp.sum(jnp.where(hit, rank, 0.0), axis=1, keepdims=True)
        dest_acc = jnp.where(lane_o == k, ik * float(MOE_CAP) + pos, dest_acc)
    dest_i = dest_acc.astype(jnp.int32)
    dest_ref[...] = dest_i
    dest_v[...] = dest_i
    to_smem = pltpu.make_async_copy(dest_v, dest_s, sem_d)
    to_smem.start()
    to_smem.wait()

    slot = lax.rem(i, 2)

    def wait_rows(s):
        for _ in range(TOP_K):
            pltpu.make_async_copy(xpk.at[s], xs_ref.at[pl.ds(0, tm * ROW_TILE)], sem_rows.at[s]).wait()

    @pl.when(i >= 2)
    def _():
        wait_rows(slot)

    _store_row_tiles(xpk.at[slot], _pack_rows(h2))

    def issue(t, carry):
        for k in range(TOP_K):
            pltpu.make_async_copy(xpk.at[slot, _tile_rows(t)], xs_ref.at[_tile_rows(dest_s[t, k])],
                                  sem_rows.at[slot]).start(priority=k % 2)
        return carry

    lax.fori_loop(0, tm, issue, 0)

    @pl.when(i == last)
    def _():
        wait_rows(slot)

        @pl.when(i >= 1)
        def _():
            wait_rows(1 - slot)


def _outproj_call(ya, yb, yc, yd, w_out_bf, x, ada, prm, sh_w, cnt_in, layer, rows_per_b, tm, xs_prev=None):
    m = x.shape[0]
    r = ada.shape[1]
    aliased = xs_prev is not None

    def full(arr):
        nd = arr.ndim
        return pl.BlockSpec(arr.shape, lambda i: (0,) * nd)

    def resident(shape):
        return pl.BlockSpec((None,) + shape, lambda i: (layer, 0, 0), pipeline_mode=pl.Buffered(1))

    grp = pl.BlockSpec((tm, W_GROUP), lambda i: (i, 0))
    row = pl.BlockSpec((tm, D_MODEL), lambda i: (i, 0))
    lane_out = pl.BlockSpec((tm, LANES), lambda i: (i, 0))
    cnt_spec = pl.BlockSpec((1, N_EXPERTS), lambda i: (0, 0))

    def mod(colidx):
        return pl.BlockSpec((None, r, D_MODEL), _mod_spec(rows_per_b, tm, colidx))

    small = [prm["ln1_g"], prm["ln1_b"], prm["router_w_bf"], prm["router_bias"]]
    in_specs = ([grp, grp, grp, grp, resident((D_MODEL, D_MODEL)), row, mod(2), mod(4), mod(3)]
                + [full(a) for a in small]
                + [resident((D_MODEL, D_EXPERT)), resident((D_MODEL, D_EXPERT)), resident((D_EXPERT, D_MODEL)), cnt_spec])
    args = [ya, yb, yc, yd, w_out_bf, x, ada, ada, ada, *small, *sh_w, cnt_in]
    if aliased:
        in_specs.append(pl.BlockSpec(memory_space=pl.ANY))
        args.append(xs_prev)
    return pl.pallas_call(
        functools.partial(_outproj_kernel, aliased=aliased),
        out_shape=(jax.ShapeDtypeStruct((m, D_MODEL), F32), jax.ShapeDtypeStruct((m, D_MODEL), F32),
                   jax.ShapeDtypeStruct((m, LANES), F32), jax.ShapeDtypeStruct((m, LANES), jnp.int32),
                   jax.ShapeDtypeStruct((1, N_EXPERTS), F32), jax.ShapeDtypeStruct((MOE_SLOTS * ROW_TILE, LANES), U32)),
        grid=(m // tm,),
        in_specs=in_specs,
        out_specs=(row, row, lane_out, lane_out, cnt_spec, pl.BlockSpec(memory_space=pl.ANY)),
        scratch_shapes=[pltpu.VMEM((2, tm * ROW_TILE, LANES), U32), pltpu.VMEM((tm, LANES), jnp.int32),
                        pltpu.SMEM((tm, LANES), jnp.int32), pltpu.VMEM((1, N_EXPERTS), F32),
                        pltpu.SemaphoreType.DMA((2,)), pltpu.SemaphoreType.DMA],
        input_output_aliases={len(args) - 1: 5} if aliased else {},
        compiler_params=_cparams(("arbitrary",)),
        name="outproj_route_dispatch",
    )(*args)


def _expert_plan(counts):
    experts = jnp.arange(N_EXPERTS, dtype=jnp.int32)
    nblk = (counts + MOE_BM - 1) // MOE_BM
    run_end = jnp.cumsum(nblk)
    run_start = run_end - nblk
    bi = jnp.arange(MOE_NB, dtype=jnp.int32)
    e_of = jnp.minimum(jnp.sum((run_end[None, :] <= bi[:, None]).astype(jnp.int32), axis=1), N_EXPERTS - 1)
    onehot = (e_of[:, None] == experts[None, :]).astype(jnp.int32)
    pick = lambda v: jnp.sum(onehot * v[None, :], axis=1)
    b_in = bi - pick(run_start)
    nonempty = nblk > 0
    later = nonempty[None, :] & (experts[None, :] > experts[:, None])
    nxt = jnp.min(jnp.where(later, experts[None, :], N_EXPERTS), axis=1)
    nxt = jnp.where(nxt == N_EXPERTS, -1, nxt)
    run_rank = jnp.cumsum(nonempty.astype(jnp.int32)) - 1
    i32 = lambda v: v.astype(jnp.int32)
    return (i32(e_of), i32(e_of * MOE_CAP_BLKS + b_in), i32(jnp.clip(pick(counts) - b_in * MOE_BM, 0, MOE_BM)),
            i32(b_in == 0), i32(pick(run_rank) & 1), i32(pick(nxt)), i32(run_end[-1:]))


def _expert_kernel(be_ref, bo_ref, nv_ref, bf_ref, bs_ref, bn_ref, nu_ref, x_ref, wg_hbm, wu_hbm, wd_hbm, o_ref,
                   wg_f, wu_f, wd_f, wg_s, wu_s, wd_s, sem, *, layer):
    i = pl.program_id(0)

    def weight_copies(e, slot):
        return (pltpu.make_async_copy(wg_hbm.at[layer, e], wg_f.at[slot], sem.at[slot]),
                pltpu.make_async_copy(wu_hbm.at[layer, e], wu_f.at[slot], sem.at[slot]),
                pltpu.make_async_copy(wd_hbm.at[layer, e], wd_f.at[slot], sem.at[slot]))

    @pl.when(i == 0)
    def _():
        for cp in weight_copies(be_ref[0], 0):
            cp.start()

    @pl.when(i < nu_ref[0])
    def _():
        @pl.when(bf_ref[i] == 1)
        def _():
            slot = bs_ref[i]
            for cp in weight_copies(be_ref[i], slot):
                cp.wait()
            nxt = bn_ref[i]

            @pl.when(nxt >= 0)
            def _():
                for cp in weight_copies(nxt, 1 - slot):
                    cp.start()

            wg_s[...] = wg_f[slot].astype(BF16)
            wu_s[...] = wu_f[slot].astype(BF16)
            wd_s[...] = wd_f[slot].astype(BF16)

        rows = lax.broadcasted_iota(jnp.int32, (MOE_BM, 1), 0)
        a, b = _unpack_rows(jnp.where(rows < nv_ref[i], _load_row_tiles(x_ref, MOE_BM), U32(0)))
        x = jnp.concatenate([a.astype(BF16), b.astype(BF16)], axis=1)
        g = _dot(x, wg_s[...])
        u = _dot(x, wu_s[...])
        _store_row_tiles(o_ref, _pack_rows(_dot((_silu(g) * u).astype(BF16), wd_s[...])))


def _expert_call(plan, xs, w_gate, w_up, w_down, layer):
    def rows_map(i, be, bo, nv, bf, bs, bn, nu):
        return (bo[jnp.minimum(i, nu[0] - 1)], 0)

    rows = pl.BlockSpec((MOE_BM * ROW_TILE, LANES), rows_map)
    hbm = pl.BlockSpec(memory_space=pl.ANY)
    grid_spec = pltpu.PrefetchScalarGridSpec(
        num_scalar_prefetch=7,
        grid=(MOE_NB,),
        in_specs=[rows, hbm, hbm, hbm],
        out_specs=rows,
        scratch_shapes=[pltpu.VMEM((2, D_MODEL, D_EXPERT), F32), pltpu.VMEM((2, D_MODEL, D_EXPERT), F32),
                        pltpu.VMEM((2, D_EXPERT, D_MODEL), F32),
                        pltpu.VMEM((D_MODEL, D_EXPERT), BF16), pltpu.VMEM((D_MODEL, D_EXPERT), BF16),
                        pltpu.VMEM((D_EXPERT, D_MODEL), BF16), pltpu.SemaphoreType.DMA((2,))],
    )
    return pl.pallas_call(
        functools.partial(_expert_kernel, layer=layer),
        out_shape=jax.ShapeDtypeStruct((MOE_SLOTS * ROW_TILE, LANES), U32),
        grid_spec=grid_spec,
        compiler_params=_cparams(("arbitrary",)),
        name="moe_experts",
    )(*plan, xs, w_gate, w_up, w_down)


def _combine_kernel(dest_ref, rows_ref, gate_ref, sh_ref, x1_ref, g2_ref, lg_ref, lb_ref, o_ref, gbuf, sem):
    tm = x1_ref.shape[0]
    i = pl.program_id(0)
    slot = lax.rem(i, 2)

    def issue(step, s):
        base = step * tm

        def body(t, carry):
            for k in range(TOP_K):
                d = dest_ref[(base + t) * TOP_K + k]
                pltpu.make_async_copy(rows_ref.at[_tile_rows(d)], gbuf.at[s, k, _tile_rows(t)],
                                      sem.at[s]).start(priority=k % 2)
            return carry

        lax.fori_loop(0, tm, body, 0)

    @pl.when(i == 0)
    def _():
        issue(0, 0)

    @pl.when(i + 1 < pl.num_programs(0))
    def _():
        issue(i + 1, 1 - slot)

    for k in range(TOP_K):
        pltpu.make_async_copy(rows_ref.at[pl.ds(0, tm * ROW_TILE)], gbuf.at[slot, k], sem.at[slot]).wait()
    gate = gate_ref[...]
    sh = sh_ref[...]
    lo = sh[:, :HALF]
    hi = sh[:, HALF:]
    for k in range(TOP_K):
        a, b = _unpack_rows(_load_row_tiles(gbuf.at[slot, k], tm))
        gk = gate[:, k:k + 1]
        lo = lo + gk * a
        hi = hi + gk * b
    moe = jnp.concatenate([lo, hi], axis=1)
    o_ref[...] = _layer_norm(ALPHA * x1_ref[...] + g2_ref[...] * moe, lg_ref[...], lb_ref[...])


def _combine_call(dest, out_rows, gate, sh, x1, ada, prm, rows_per_b, tm):
    m = x1.shape[0]
    r = ada.shape[1]
    row = pl.BlockSpec((tm, D_MODEL), lambda i, d: (i, 0))
    vec = pl.BlockSpec((1, D_MODEL), lambda i, d: (0, 0))
    grid_spec = pltpu.PrefetchScalarGridSpec(
        num_scalar_prefetch=1,
        grid=(m // tm,),
        in_specs=[pl.BlockSpec(memory_space=pl.ANY),
                  pl.BlockSpec((tm, LANES), lambda i, d: (i, 0)), row, row,
                  pl.BlockSpec((None, r, D_MODEL), _mod_spec(rows_per_b, tm, 5)), vec, vec],
        out_specs=row,
        scratch_shapes=[pltpu.VMEM((2, TOP_K, tm * ROW_TILE, LANES), U32), pltpu.SemaphoreType.DMA((2,))],
    )
    return pl.pallas_call(
        _combine_kernel,
        out_shape=jax.ShapeDtypeStruct((m, D_MODEL), F32),
        grid_spec=grid_spec,
        compiler_params=_cparams(("arbitrary",)),
        name="moe_combine",
    )(dest, out_rows, gate, sh, x1, ada, prm["ln2_g"], prm["ln2_b"])


def _block_diag(w):
    eye = jnp.eye(LRU_BLOCKS, dtype=w.dtype)
    full = jnp.einsum("hcd,hg->hcgd", w, eye).reshape(W_GROUP, W_GROUP)
    return full.astype(BF16)


def _rope_tables(pos):
    half = HEAD_DIM // 2
    freqs = ROPE_THETA ** (-jnp.arange(half, dtype=F32) / half)
    ang = pos.astype(F32)[:, None] * freqs[None, :]
    cos = jnp.cos(ang)
    sin = jnp.sin(ang)
    cos_t = jnp.concatenate([cos, cos, cos, cos], axis=1)
    sin_t = jnp.concatenate([-sin, sin, -sin, sin], axis=1)
    return cos_t, sin_t


def kernel(x_prompt, x_sample, cache_k, cache_v, state_conv_a, state_lru, state_conv_c, state_conv_d, page_table, c_prompt, c_sample, w_ada, b_ada, w_in, conv_a_w, conv_a_b, lru_wr, lru_br, lru_wi, lru_bi, lru_lambda, lam_q1, lam_k1, lam_q2, lam_k2, subln_g, conv_c_w, conv_c_b, ln_c_g, ln_c_b, conv_d_w, w_out, ln1_g, ln1_b, ln2_g, ln2_b, router_w, router_bias, exp_w_gate, exp_w_up, exp_w_down, sh_w_gate, sh_w_up, sh_w_down):
    n_pool = cache_k.shape[1]
    c_all = jnp.concatenate([c_prompt, c_sample, jnp.zeros((16 - BATCH - DEC_BATCH, D_MODEL), F32)], axis=0)
    ada_all = _ada_call(c_all, w_ada, b_ada)
    w_in_bf = w_in.astype(BF16)
    w_out_bf = w_out.astype(BF16)
    sh_wg_bf = sh_w_gate.astype(BF16)
    sh_wu_bf = sh_w_up.astype(BF16)
    sh_wd_bf = sh_w_down.astype(BF16)
    sh_w = (sh_wg_bf, sh_wu_bf, sh_wd_bf)
    cnt0 = jnp.zeros((1, N_EXPERTS), F32)
    cache_k4 = cache_k.reshape(DEPTH, n_pool, PAGE_SIZE * N_HEADS, LANES)
    cache_v4 = cache_v.reshape(DEPTH, n_pool, PAGE_SIZE * N_HEADS, LANES)
    cos_p, sin_p = _rope_tables(jnp.arange(SEQ, dtype=jnp.int32))
    cos_s, sin_s = _rope_tables(jnp.full((1,), PAST_LEN, jnp.int32))
    lane = jnp.arange(LANES)
    comp_mask = jnp.stack([lane < HEAD_DIM, lane >= HEAD_DIM]).astype(F32)

    xp = x_prompt.reshape(T_PROMPT, D_MODEL)
    xs = x_sample.reshape(DEC_BATCH, D_MODEL)
    outs = [[] for _ in range(12)]
    for l in range(DEPTH):
        lam_init = 0.8 - 0.6 * math.exp(-0.3 * l)
        prm = dict(
            conv_a_w=conv_a_w[l], conv_a_b=conv_a_b[l][None], wr=_block_diag(lru_wr[l]), br=lru_br[l][None],
            wi=_block_diag(lru_wi[l]), bi=lru_bi[l][None], lam=lru_lambda[l][None],
            conv_c_w=conv_c_w[l], conv_c_b=conv_c_b[l][None], ln_c_g=ln_c_g[l][None], ln_c_b=ln_c_b[l][None],
            conv_d_w=conv_d_w[l],
            lam_q1=lam_q1[l][None], lam_k1=lam_k1[l][None], lam_q2=lam_q2[l][None], lam_k2=lam_k2[l][None],
            subln_g=subln_g[l][None], subln_g_col=subln_g[l][:, None], lam_init=jnp.full((1, 1), lam_init, F32),
            ln1_g=ln1_g[l][None], ln1_b=ln1_b[l][None], ln2_g=ln2_g[l][None], ln2_b=ln2_b[l][None],
            router_w_bf=router_w[l].astype(BF16), router_bias=router_bias[l][None],
        )
        ada_p = ada_all[l, :BATCH].reshape(BATCH, 1, 6 * D_MODEL)
        ada_s = ada_all[l, BATCH:BATCH + DEC_BATCH].reshape(1, DEC_BATCH, 6 * D_MODEL)

        proj_p = _inproj_call(xp, ada_p, w_in_bf, l, SEQ, 1024, 512).reshape(BATCH, SEQ, D_IN)
        ya, yc, yd, q_p, k_p, sa, sh, sc, sd = _mixer_prompt_call(proj_p, cos_p, sin_p, prm)
        yb = _flash_call(q_p, k_p, proj_p, prm)
        outs[0].append(k_p)
        outs[1].append(proj_p[:, :, 4 * W_GROUP:5 * W_GROUP])
        outs[2].append(sa)
        outs[3].append(sh.reshape(BATCH, W_GROUP))
        outs[4].append(sc)
        outs[5].append(sd)

        proj_s = _inproj_call(xs, ada_s, w_in_bf, l, DEC_BATCH, DEC_BATCH, 512)
        ya_s, yc_s, yd_s, q_s, k_s, sa_s, sh_s, sc_s, sd_s = _mixer_sample_call(
            proj_s, cos_s, sin_s, prm,
            state_conv_a[l].transpose(1, 0, 2), state_lru[l], state_conv_c[l].transpose(1, 0, 2),
            state_conv_d[l].transpose(1, 0, 2))
        v_s = proj_s[:, 4 * W_GROUP:5 * W_GROUP]
        q4 = q_s.reshape(DEC_BATCH, 1, N_HEADS, LANES) * (HEAD_DIM ** -0.5)
        q8 = (q4 * comp_mask[None, :, None, :]).reshape(DEC_BATCH, 2 * N_HEADS, LANES).astype(BF16)
        pad4 = jnp.zeros((DEC_BATCH, N_HEADS, LANES), F32)
        k_new8 = jnp.concatenate([k_s.reshape(DEC_BATCH, N_HEADS, LANES), pad4], axis=1)
        v_new8 = jnp.concatenate([v_s.reshape(DEC_BATCH, N_HEADS, LANES), pad4], axis=1)
        yb_s = _decode_call(q8, k_new8, v_new8, cache_k4, cache_v4, page_table, prm, l)
        yb_s = yb_s.reshape(DEC_BATCH, W_GROUP).astype(BF16)
        outs[6].append(k_s.reshape(DEC_BATCH, 1, N_HEADS, 2 * HEAD_DIM))
        outs[7].append(v_s.reshape(DEC_BATCH, 1, N_HEADS, 2 * HEAD_DIM))
        outs[8].append(sa_s.transpose(1, 0, 2))
        outs[9].append(sh_s)
        outs[10].append(sc_s.transpose(1, 0, 2))
        outs[11].append(sd_s.transpose(1, 0, 2))

        flat = lambda t: t.reshape(T_PROMPT, W_GROUP)
        x1_p, shared_p, gate_p, dest_p, cnt_p, x_sorted = _outproj_call(
            flat(ya), flat(yb), flat(yc), flat(yd), w_out_bf, xp, ada_p, prm, sh_w, cnt0, l, SEQ, 256)
        x1_s, shared_s, gate_s, dest_s, cnt_all, x_sorted = _outproj_call(
            ya_s, yb_s, yc_s, yd_s, w_out_bf, xs, ada_s, prm, sh_w, cnt_p, l, DEC_BATCH, DEC_BATCH, xs_prev=x_sorted)

        plan = _expert_plan(cnt_all[0].astype(jnp.int32))
        out_rows = _expert_call(plan, x_sorted, exp_w_gate, exp_w_up, exp_w_down, l)
        xp = _combine_call(dest_p[:, :TOP_K].reshape(-1), out_rows, gate_p, shared_p, x1_p, ada_p, prm, SEQ, 128)
        xs = _combine_call(dest_s[:, :TOP_K].reshape(-1), out_rows, gate_s, shared_s, x1_s, ada_s, prm, DEC_BATCH, DEC_BATCH)

    st = [jnp.stack(o, axis=0) for o in outs]
    kv_shape = (DEPTH, BATCH, SEQ // PAGE_SIZE, PAGE_SIZE, N_HEADS, 2 * HEAD_DIM)
    return (xp.reshape(BATCH, SEQ, D_MODEL), xs.reshape(DEC_BATCH, 1, D_MODEL),
            st[0].reshape(kv_shape), st[1].reshape(kv_shape), st[2], st[3], st[4], st[5],
            st[6], st[7], st[8], st[9], st[10], st[11])
```

```python
import functools
import math

import jax
import jax.numpy as jnp
from jax import lax
from jax.experimental import pallas as pl
from jax.experimental.pallas import tpu as pltpu

F32 = jnp.float32
BF16 = jnp.bfloat16

D_MODEL = 2048
BATCH = 2
SEQ = 4096
DEPTH = 4
DEC_BATCH = 8
PAST_LEN = 16384
PAGE_SIZE = 128
W_GROUP = 512
N_IN_PARTS = 10
D_IN = N_IN_PARTS * W_GROUP
CONV_A = 4
LRU_BLOCKS = 8
LRU_BLOCK_DIM = W_GROUP // LRU_BLOCKS
LRU_C = 8.0
HEAD_DIM = 64
N_HEADS = 4
ROPE_THETA = 10000.0
CONV_C = 31
CONV_D = 3
N_EXPERTS = 64
TOP_K = 8
D_EXPERT = 512
ROUTE_SCALE = 2.5
LN_EPS = 1e-5
ALPHA = (2.0 * DEPTH) ** 0.25
NEG_INF = -1e30

LANES = 128
VMEM_LIMIT = 56 * 1024 * 1024

T_PROMPT = BATCH * SEQ
T_ALL = T_PROMPT + DEC_BATCH
MOE_BM = 256
MOE_NB = -(-T_ALL * TOP_K // MOE_BM) + N_EXPERTS
N_PAGES = PAST_LEN // PAGE_SIZE
DEC_G = 16


def _cparams(sem, **kw):
    return pltpu.CompilerParams(dimension_semantics=sem, vmem_limit_bytes=VMEM_LIMIT, **kw)


def _dot(a, b):
    return jnp.dot(a, b, preferred_element_type=F32)


def _dot_nt(a, b):
    return lax.dot_general(a, b, (((1,), (1,)), ((), ())), preferred_element_type=F32)


def _sigmoid(x):
    return 1.0 / (1.0 + jnp.exp(-x))


def _silu(x):
    return x * _sigmoid(x)


def _layer_norm(z, g, b):
    mu = jnp.mean(z, axis=-1, keepdims=True)
    zc = z - mu
    var = jnp.mean(zc * zc, axis=-1, keepdims=True)
    return zc * lax.rsqrt(var + LN_EPS) * g + b


def _ada_kernel(c_ref, w_ref, b_ref, o_ref):
    c = c_ref[...]
    o_ref[...] = _dot(_silu(c).astype(BF16), w_ref[...].astype(BF16)) + b_ref[...]


def _ada_call(c_all, w_ada, b_ada):
    r = c_all.shape[0]
    tn = 1024
    return pl.pallas_call(
        _ada_kernel,
        out_shape=jax.ShapeDtypeStruct((DEPTH, r, 6 * D_MODEL), F32),
        grid=(DEPTH, 6 * D_MODEL // tn),
        in_specs=[
            pl.BlockSpec((r, D_MODEL), lambda l, n: (0, 0)),
            pl.BlockSpec((None, D_MODEL, tn), lambda l, n: (l, 0, n)),
            pl.BlockSpec((None, 1, tn), lambda l, n: (l, 0, n)),
        ],
        out_specs=pl.BlockSpec((None, r, tn), lambda l, n: (l, 0, n)),
        compiler_params=_cparams(("arbitrary", "arbitrary")),
        name="ada",
    )(c_all, w_ada, b_ada.reshape(DEPTH, 1, 6 * D_MODEL))


def _inproj_kernel(x_ref, sc_ref, sh_ref, w_ref, o_ref, h_scr):
    @pl.when(pl.program_id(1) == 0)
    def _():
        h_scr[...] = (x_ref[...] * (1.0 + sc_ref[...]) + sh_ref[...]).astype(BF16)

    o_ref[...] = _dot(h_scr[...], w_ref[...])


def _mod_spec(rows_per_b, tm, col):
    def imap(m, *_):
        return ((m * tm) // rows_per_b, 0, col)
    return imap


def _inproj_call(x, ada, w_in_bf, layer, rows_per_b, tm, tn):
    m = x.shape[0]
    r = ada.shape[1]
    return pl.pallas_call(
        _inproj_kernel,
        out_shape=jax.ShapeDtypeStruct((m, D_IN), F32),
        grid=(m // tm, D_IN // tn),
        in_specs=[
            pl.BlockSpec((tm, D_MODEL), lambda i, n: (i, 0)),
            pl.BlockSpec((None, r, D_MODEL), _mod_spec(rows_per_b, tm, 1)),
            pl.BlockSpec((None, r, D_MODEL), _mod_spec(rows_per_b, tm, 0)),
            pl.BlockSpec((None, D_MODEL, tn), lambda i, n: (layer, 0, n)),
        ],
        out_specs=pl.BlockSpec((tm, tn), lambda i, n: (i, n)),
        scratch_shapes=[pltpu.VMEM((tm, D_MODEL), BF16)],
        compiler_params=_cparams(("arbitrary", "arbitrary")),
        name="inproj",
    )(x, ada, ada, w_in_bf)


def _softplus(z):
    return jnp.maximum(z, 0.0) + jnp.log1p(jnp.exp(-jnp.abs(z)))


def _gelu_tanh(x):
    return 0.5 * x * (1.0 + jnp.tanh(math.sqrt(2.0 / math.pi) * (x + 0.044715 * (x * x * x))))


def _lru_gates(xa, wr, br, wi, bi, lam):
    xb = xa.astype(BF16)
    r = _sigmoid(_dot(xb, wr) + br)
    i = _sigmoid(_dot(xb, wi) + bi)
    log_a = -LRU_C * r * _softplus(-lam)
    a = jnp.exp(log_a)
    u = jnp.sqrt(jnp.tanh(-log_a) * (1.0 + a * a)) * (i * xa)
    return a, u


def _rope(x, cos_t, sin_t):
    lane = lax.broadcasted_iota(jnp.int32, x.shape, 1)
    first_half = (lane & (HEAD_DIM // 2)) == 0
    w = x.shape[1]
    partner = jnp.where(first_half, pltpu.roll(x, w - HEAD_DIM // 2, axis=1), pltpu.roll(x, HEAD_DIM // 2, axis=1))
    return x * cos_t + partner * sin_t


def _tile4(t):
    return jnp.concatenate([t, t, t, t], axis=1)


def _conv_from_buf(buf_ref, w_ref, hist, n_taps, rows, rb):
    outs = []
    for r0 in range(0, rows, rb):
        acc = None
        for j in range(n_taps):
            term = buf_ref[pl.ds(hist - (n_taps - 1) + j + r0, rb), :] * w_ref[pl.ds(j, 1), :]
            acc = term if acc is None else acc + term
        outs.append(acc)
    return jnp.concatenate(outs, axis=0) if len(outs) > 1 else outs[0]


def _linear_scan(a, b):
    n = a.shape[0]
    rows = lax.broadcasted_iota(jnp.int32, a.shape, 0)
    d = 1
    while d < n:
        keep = rows >= d
        a_sh = jnp.where(keep, pltpu.roll(a, d, axis=0), 1.0)
        b_sh = jnp.where(keep, pltpu.roll(b, d, axis=0), 0.0)
        b = a * b_sh + b
        a = a * a_sh
        d *= 2
    return a, b


MIX_TC = 256
HIST_A = 8
HIST_C = 32
HIST_D = 8
CONV_RB = 32


def _store_head_rows(ref, x):
    n = x.shape[0]
    for h in range(N_HEADS):
        ref[pl.ds(h, n, stride=N_HEADS), :] = x[:, h * LANES:(h + 1) * LANES]


def _mixer_prompt_kernel(ax_ref, ag_ref, bq_ref, bk_ref, bv_ref, cv_ref, cg_ref, db_ref, dc_ref, dx_ref,
                         cos_ref, sin_ref, caw_ref, cab_ref, wr_ref, br_ref, wi_ref, bi_ref, lam_ref,
                         ccw_ref, ccb_ref, lng_ref, lnb_ref, cdw_ref,
                         ya_ref, yc_ref, yd_ref, q_ref, k_ref, kh_ref, vh_ref, sa_ref, sh_ref, sc_ref, sd_ref,
                         xa_buf, uc_buf, dd_buf, h_buf):
    tc = MIX_TC

    @pl.when(pl.program_id(1) == 0)
    def _():
        xa_buf[pl.ds(0, HIST_A), :] = jnp.zeros((HIST_A, W_GROUP), F32)
        uc_buf[pl.ds(0, HIST_C), :] = jnp.zeros((HIST_C, W_GROUP), F32)
        dd_buf[pl.ds(0, HIST_D), :] = jnp.zeros((HIST_D, W_GROUP), F32)
        h_buf[...] = jnp.zeros((1, W_GROUP), F32)

    xa_buf[pl.ds(HIST_A, tc), :] = ax_ref[...]
    xa = _conv_from_buf(xa_buf, caw_ref, HIST_A, CONV_A, tc, CONV_RB) + cab_ref[...]
    a, u = _lru_gates(xa, wr_ref[...], br_ref[...], wi_ref[...], bi_ref[...], lam_ref[...])
    a_cum, b_cum = _linear_scan(a, u)
    h = a_cum * h_buf[...] + b_cum
    ya_ref[...] = (_gelu_tanh(ag_ref[...]) * h).astype(BF16)
    h_last = h[tc - 1:tc, :]
    h_buf[...] = h_last
    sh_ref[...] = h_last
    sa_ref[...] = xa_buf[pl.ds(HIST_A + tc - (CONV_A - 1), CONV_A - 1), :]
    xa_buf[pl.ds(0, HIST_A), :] = xa_buf[pl.ds(tc, HIST_A), :]

    uc_buf[pl.ds(HIST_C, tc), :] = cv_ref[...] * _sigmoid(cg_ref[...])
    yc = _conv_from_buf(uc_buf, ccw_ref, HIST_C, CONV_C, tc, CONV_RB) + ccb_ref[...]
    yc_ref[...] = _silu(_layer_norm(yc, lng_ref[...], lnb_ref[...])).astype(BF16)
    sc_ref[...] = uc_buf[pl.ds(HIST_C + tc - (CONV_C - 1), CONV_C - 1), :]
    uc_buf[pl.ds(0, HIST_C), :] = uc_buf[pl.ds(tc, HIST_C), :]

    dd_buf[pl.ds(HIST_D, tc), :] = dc_ref[...] * dx_ref[...]
    yd = _conv_from_buf(dd_buf, cdw_ref, HIST_D, CONV_D, tc, CONV_RB)
    yd_ref[...] = (db_ref[...] * yd).astype(BF16)
    sd_ref[...] = dd_buf[pl.ds(HIST_D + tc - (CONV_D - 1), CONV_D - 1), :]
    dd_buf[pl.ds(0, HIST_D), :] = dd_buf[pl.ds(tc, HIST_D), :]

    cos_t = _tile4(cos_ref[...])
    sin_t = _tile4(sin_ref[...])
    q_ref[...] = (_rope(bq_ref[...], cos_t, sin_t) * (HEAD_DIM ** -0.5)).astype(BF16)
    k_rot = _rope(bk_ref[...], cos_t, sin_t)
    k_ref[...] = k_rot
    _store_head_rows(kh_ref, k_rot)
    _store_head_rows(vh_ref, bv_ref[...])


def _mixer_prompt_call(proj, cos_t, sin_t, prm):
    tc = MIX_TC

    def col(j):
        return pl.BlockSpec((None, tc, W_GROUP), lambda b, c: (b, c, j))

    def full(arr):
        nd = arr.ndim
        return pl.BlockSpec(arr.shape, lambda b, c: (0,) * nd)

    small = [prm["conv_a_w"], prm["conv_a_b"], prm["wr"], prm["br"], prm["wi"], prm["bi"], prm["lam"],
             prm["conv_c_w"], prm["conv_c_b"], prm["ln_c_g"], prm["ln_c_b"], prm["conv_d_w"]]
    tok = pl.BlockSpec((None, tc, W_GROUP), lambda b, c: (b, c, 0))

    def state(n):
        return pl.BlockSpec((None, n, W_GROUP), lambda b, c: (b, 0, 0))

    out_shape = (
        jax.ShapeDtypeStruct((BATCH, SEQ, W_GROUP), BF16),
        jax.ShapeDtypeStruct((BATCH, SEQ, W_GROUP), BF16),
        jax.ShapeDtypeStruct((BATCH, SEQ, W_GROUP), BF16),
        jax.ShapeDtypeStruct((BATCH, SEQ, W_GROUP), BF16),
        jax.ShapeDtypeStruct((BATCH, SEQ, W_GROUP), F32),
        jax.ShapeDtypeStruct((BATCH, SEQ * N_HEADS, LANES), F32),
        jax.ShapeDtypeStruct((BATCH, SEQ * N_HEADS, LANES), F32),
        jax.ShapeDtypeStruct((BATCH, CONV_A - 1, W_GROUP), F32),
        jax.ShapeDtypeStruct((BATCH, 1, W_GROUP), F32),
        jax.ShapeDtypeStruct((BATCH, CONV_C - 1, W_GROUP), F32),
        jax.ShapeDtypeStruct((BATCH, CONV_D - 1, W_GROUP), F32),
    )
    heads = pl.BlockSpec((None, tc * N_HEADS, LANES), lambda b, c: (b, c, 0))
    return pl.pallas_call(
        _mixer_prompt_kernel,
        out_shape=out_shape,
        grid=(BATCH, SEQ // tc),
        in_specs=[col(0), col(1), col(2), col(3), col(4), col(5), col(6), col(7), col(8), col(9),
                  pl.BlockSpec((tc, LANES), lambda b, c: (c, 0)),
                  pl.BlockSpec((tc, LANES), lambda b, c: (c, 0))] + [full(a) for a in small],
        out_specs=(tok, tok, tok, tok, tok, heads, heads,
                   state(CONV_A - 1), state(1), state(CONV_C - 1), state(CONV_D - 1)),
        scratch_shapes=[pltpu.VMEM((HIST_A + tc, W_GROUP), F32), pltpu.VMEM((HIST_C + tc, W_GROUP), F32),
                        pltpu.VMEM((HIST_D + tc, W_GROUP), F32), pltpu.VMEM((1, W_GROUP), F32)],
        compiler_params=_cparams(("arbitrary", "arbitrary")),
        name="mixer_prompt",
    )(*([proj] * 10), cos_t, sin_t, *small)


def _mixer_sample_kernel(ax_ref, ag_ref, bq_ref, bk_ref, cv_ref, cg_ref, db_ref, dc_ref, dx_ref,
                         cos_ref, sin_ref, caw_ref, cab_ref, wr_ref, br_ref, wi_ref, bi_ref, lam_ref,
                         ccw_ref, ccb_ref, lng_ref, lnb_ref, cdw_ref,
                         ba_ref, h0_ref, bc_ref, bd_ref,
                         ya_ref, yc_ref, yd_ref, q_ref, k_ref, sa_ref, sh_ref, sc_ref, sd_ref):
    ax = ax_ref[...]
    xa = cab_ref[...] + caw_ref[pl.ds(CONV_A - 1, 1), :] * ax
    for j in range(CONV_A - 1):
        xa = xa + caw_ref[pl.ds(j, 1), :] * ba_ref[j]
    a, u = _lru_gates(xa, wr_ref[...], br_ref[...], wi_ref[...], bi_ref[...], lam_ref[...])
    h = a * h0_ref[...] + u
    ya_ref[...] = (_gelu_tanh(ag_ref[...]) * h).astype(BF16)
    sh_ref[...] = h
    for j in range(CONV_A - 2):
        sa_ref[j] = ba_ref[j + 1]
    sa_ref[CONV_A - 2] = ax

    uc = cv_ref[...] * _sigmoid(cg_ref[...])
    yc = ccb_ref[...] + ccw_ref[pl.ds(CONV_C - 1, 1), :] * uc
    for j in range(CONV_C - 1):
        yc = yc + ccw_ref[pl.ds(j, 1), :] * bc_ref[j]
    yc_ref[...] = _silu(_layer_norm(yc, lng_ref[...], lnb_ref[...])).astype(BF16)
    for j in range(CONV_C - 2):
        sc_ref[j] = bc_ref[j + 1]
    sc_ref[CONV_C - 2] = uc

    dd = dc_ref[...] * dx_ref[...]
    yd = cdw_ref[pl.ds(CONV_D - 1, 1), :] * dd
    for j in range(CONV_D - 1):
        yd = yd + cdw_ref[pl.ds(j, 1), :] * bd_ref[j]
    yd_ref[...] = (db_ref[...] * yd).astype(BF16)
    for j in range(CONV_D - 2):
        sd_ref[j] = bd_ref[j + 1]
    sd_ref[CONV_D - 2] = dd

    cos_t = _tile4(cos_ref[...])
    sin_t = _tile4(sin_ref[...])
    q_ref[...] = _rope(bq_ref[...], cos_t, sin_t)
    k_ref[...] = _rope(bk_ref[...], cos_t, sin_t)


def _mixer_sample_call(proj, cos_t, sin_t, prm, buf_a, h0, buf_c, buf_d):
    nb = DEC_BATCH

    def col(j):
        return pl.BlockSpec((nb, W_GROUP), lambda i: (0, j))

    def full(arr):
        nd = arr.ndim
        return pl.BlockSpec(arr.shape, lambda i: (0,) * nd)

    small = [prm["conv_a_w"], prm["conv_a_b"], prm["wr"], prm["br"], prm["wi"], prm["bi"], prm["lam"],
             prm["conv_c_w"], prm["conv_c_b"], prm["ln_c_g"], prm["ln_c_b"], prm["conv_d_w"]]
    states = [buf_a, h0, buf_c, buf_d]
    tok_bf = jax.ShapeDtypeStruct((nb, W_GROUP), BF16)
    tok_f = jax.ShapeDtypeStruct((nb, W_GROUP), F32)
    out_shape = (tok_bf, tok_bf, tok_bf, tok_f, tok_f,
                 jax.ShapeDtypeStruct(buf_a.shape, F32), tok_f,
                 jax.ShapeDtypeStruct(buf_c.shape, F32), jax.ShapeDtypeStruct(buf_d.shape, F32))
    return pl.pallas_call(
        _mixer_sample_kernel,
        out_shape=out_shape,
        grid=(1,),
        in_specs=[col(0), col(1), col(2), col(3), col(5), col(6), col(7), col(8), col(9),
                  full(cos_t), full(sin_t)] + [full(a) for a in small] + [full(a) for a in states],
        out_specs=tuple(pl.BlockSpec(s.shape, lambda i, nd=len(s.shape): (0,) * nd) for s in out_shape),
        compiler_params=_cparams(("arbitrary",)),
        name="mixer_sample",
    )(*([proj] * 9), cos_t, sin_t, *small, *states)


def _diff_lambda(lq1_ref, lk1_ref, lq2_ref, lk2_ref, lam_init):
    s1 = jnp.sum(lq1_ref[...] * lk1_ref[...], axis=-1, keepdims=True)
    s2 = jnp.sum(lq2_ref[...] * lk2_ref[...], axis=-1, keepdims=True)
    return jnp.exp(s1) - jnp.exp(s2) + lam_init


def _diff_finish(o1, o2, lam, g, lam_init):
    od = o1 - lam * o2
    ms = jnp.mean(od * od, axis=-1, keepdims=True)
    return od * lax.rsqrt(ms + LN_EPS) * g * (1.0 - lam_init)


FLASH_T = 512


def _flash_kernel(q_ref, k_ref, v_ref, lq1_ref, lk1_ref, lq2_ref, lk2_ref, gcol_ref, li_ref, o_ref,
                  kb_scr, vt_scr, m_scr, l_scr, acc_scr):
    t = FLASH_T
    i = pl.program_id(2)

    @pl.when(i == 0)
    def _():
        for c in range(SEQ // t):
            kb_scr[c] = k_ref[pl.ds(c * t, t), :].astype(BF16)
            vt_scr[c] = v_ref[pl.ds(c * t, t), :].T.astype(BF16)

    q = q_ref[...]
    lane = lax.broadcasted_iota(jnp.int32, q.shape, 1)
    zero = jnp.zeros_like(q)
    qq = jnp.concatenate([jnp.where(lane < HEAD_DIM, q, zero), jnp.where(lane >= HEAD_DIM, q, zero)], axis=0)
    m_scr[...] = jnp.full(m_scr.shape, NEG_INF, F32)
    l_scr[...] = jnp.zeros(l_scr.shape, F32)
    acc_scr[...] = jnp.zeros(acc_scr.shape, F32)

    def step(j, masked):
        st = _dot_nt(kb_scr[j], qq)
        if masked:
            r = lax.broadcasted_iota(jnp.int32, st.shape, 0)
            c = lax.broadcasted_iota(jnp.int32, st.shape, 1)
            st = jnp.where(r <= jnp.where(c >= t, c - t, c), st, NEG_INF)
        m_prev = m_scr[...]
        m_new = jnp.maximum(m_prev, jnp.max(st, axis=0, keepdims=True))
        alpha = jnp.exp(m_prev - m_new)
        p = jnp.exp(st - m_new)
        l_scr[...] = alpha * l_scr[...] + jnp.sum(p, axis=0, keepdims=True)
        acc_scr[...] = alpha * acc_scr[...] + _dot(vt_scr[j], p.astype(BF16))
        m_scr[...] = m_new

    def body(j, carry):
        step(j, False)
        return carry

    lax.fori_loop(0, i, body, 0)
    step(i, True)

    o = acc_scr[...] / l_scr[...]
    lam_init = li_ref[...]
    lam = _diff_lambda(lq1_ref, lk1_ref, lq2_ref, lk2_ref, lam_init)
    od = o[:, :t] - lam * o[:, t:]
    ms = jnp.mean(od * od, axis=0, keepdims=True)
    y = od * lax.rsqrt(ms + LN_EPS) * gcol_ref[...] * (1.0 - lam_init)
    o_ref[...] = y.T.astype(BF16)


def _flash_call(q, k, proj, prm):
    t = FLASH_T
    v_col0 = 4 * W_GROUP // LANES

    def full(arr):
        nd = arr.ndim
        return pl.BlockSpec(arr.shape, lambda b, h, i: (0,) * nd)

    small = [prm["lam_q1"], prm["lam_k1"], prm["lam_q2"], prm["lam_k2"], prm["subln_g_col"], prm["lam_init"]]
    return pl.pallas_call(
        _flash_kernel,
        out_shape=jax.ShapeDtypeStruct((BATCH, SEQ, W_GROUP), BF16),
        grid=(BATCH, N_HEADS, SEQ // t),
        in_specs=[pl.BlockSpec((None, t, LANES), lambda b, h, i: (b, i, h)),
                  pl.BlockSpec((None, SEQ, LANES), lambda b, h, i: (b, 0, h)),
                  pl.BlockSpec((None, SEQ, LANES), lambda b, h, i: (b, 0, v_col0 + h))] + [full(a) for a in small],
        out_specs=pl.BlockSpec((None, t, LANES), lambda b, h, i: (b, i, h)),
        scratch_shapes=[pltpu.VMEM((SEQ // t, t, LANES), BF16), pltpu.VMEM((SEQ // t, LANES, t), BF16),
                        pltpu.VMEM((1, 2 * t), F32), pltpu.VMEM((1, 2 * t), F32), pltpu.VMEM((LANES, 2 * t), F32)],
        compiler_params=_cparams(("arbitrary", "arbitrary", "arbitrary")),
        name="flash",
    )(q, k, proj, *small)


def _softmax_update(s, v, m_scr, l_scr, acc_scr):
    m_prev = m_scr[...]
    m_new = jnp.maximum(m_prev, jnp.max(s, axis=1, keepdims=True))
    alpha = jnp.exp(m_prev - m_new)
    p = jnp.exp(s - m_new)
    l_scr[...] = alpha * l_scr[...] + jnp.sum(p, axis=1, keepdims=True)
    acc_scr[...] = alpha * acc_scr[...] + _dot(p.astype(BF16), v)
    m_scr[...] = m_new


def _decode_kernel(pt_ref, q_ref, *refs):
    k_refs = refs[:DEC_G]
    v_refs = refs[DEC_G:2 * DEC_G]
    (kn_ref, vn_ref, lq1_ref, lk1_ref, lq2_ref, lk2_ref, g_ref, li_ref, o_ref, m_scr, l_scr, acc_scr) = refs[2 * DEC_G:]
    s_idx = pl.program_id(1)

    @pl.when(s_idx == 0)
    def _():
        m_scr[...] = jnp.full(m_scr.shape, NEG_INF, F32)
        l_scr[...] = jnp.zeros(l_scr.shape, F32)
        acc_scr[...] = jnp.zeros(acc_scr.shape, F32)

    q = q_ref[...]
    n_kv = PAGE_SIZE * N_HEADS
    row = lax.broadcasted_iota(jnp.int32, (2 * N_HEADS, n_kv), 0)
    col = lax.broadcasted_iota(jnp.int32, (2 * N_HEADS, n_kv), 1)
    same_head = (col & (N_HEADS - 1)) == (row & (N_HEADS - 1))
    s = jnp.concatenate([jnp.where(same_head, _dot_nt(q, k_refs[g][...].astype(BF16)), NEG_INF)
                         for g in range(DEC_G)], axis=1)
    m_prev = m_scr[...]
    m_new = jnp.maximum(m_prev, jnp.max(s, axis=1, keepdims=True))
    alpha = jnp.exp(m_prev - m_new)
    p = jnp.exp(s - m_new)
    l_scr[...] = alpha * l_scr[...] + jnp.sum(p, axis=1, keepdims=True)
    pb = p.astype(BF16)
    pv = _dot(pb[:, :n_kv], v_refs[0][...].astype(BF16))
    for g in range(1, DEC_G):
        pv = pv + _dot(pb[:, g * n_kv:(g + 1) * n_kv], v_refs[g][...].astype(BF16))
    acc_scr[...] = alpha * acc_scr[...] + pv
    m_scr[...] = m_new

    @pl.when(s_idx == pl.num_programs(1) - 1)
    def _():
        kn = kn_ref[...].astype(BF16)
        vn = vn_ref[...].astype(BF16)
        r8 = lax.broadcasted_iota(jnp.int32, (2 * N_HEADS, 2 * N_HEADS), 0)
        c8 = lax.broadcasted_iota(jnp.int32, (2 * N_HEADS, 2 * N_HEADS), 1)
        s = jnp.where(c8 == (r8 & (N_HEADS - 1)), _dot_nt(q, kn), NEG_INF)
        _softmax_update(s, vn, m_scr, l_scr, acc_scr)
        o = acc_scr[...] / l_scr[...]
        lam_init = li_ref[...]
        lam = _diff_lambda(lq1_ref, lk1_ref, lq2_ref, lk2_ref, lam_init)
        o_ref[...] = _diff_finish(o[:N_HEADS], o[N_HEADS:], lam, g_ref[...], lam_init)


def _decode_call(q8, k_new8, v_new8, cache_k4, cache_v4, page_table, prm, layer):
    n_kv = PAGE_SIZE * N_HEADS
    steps = N_PAGES // DEC_G

    def page_spec(g):
        return pl.BlockSpec((None, None, n_kv, LANES), lambda b, s, pt: (layer, pt[b, s * DEC_G + g], 0, 0))

    def full(arr):
        nd = arr.ndim
        return pl.BlockSpec(arr.shape, lambda b, s, pt: (0,) * nd)

    row8 = pl.BlockSpec((None, 2 * N_HEADS, LANES), lambda b, s, pt: (b, 0, 0))
    small = [prm["lam_q1"], prm["lam_k1"], prm["lam_q2"], prm["lam_k2"], prm["subln_g"], prm["lam_init"]]
    grid_spec = pltpu.PrefetchScalarGridSpec(
        num_scalar_prefetch=1,
        grid=(DEC_BATCH, steps),
        in_specs=[row8] + [page_spec(g) for g in range(DEC_G)] + [page_spec(g) for g in range(DEC_G)]
                 + [row8, row8] + [full(a) for a in small],
        out_specs=pl.BlockSpec((None, N_HEADS, LANES), lambda b, s, pt: (b, 0, 0)),
        scratch_shapes=[pltpu.VMEM((2 * N_HEADS, 1), F32), pltpu.VMEM((2 * N_HEADS, 1), F32),
                        pltpu.VMEM((2 * N_HEADS, LANES), F32)],
    )
    return pl.pallas_call(
        _decode_kernel,
        out_shape=jax.ShapeDtypeStruct((DEC_BATCH, N_HEADS, LANES), F32),
        grid_spec=grid_spec,
        compiler_params=_cparams(("arbitrary", "arbitrary")),
        name="decode_attn",
    )(page_table, q8, *([cache_k4] * DEC_G), *([cache_v4] * DEC_G), k_new8, v_new8, *small)


MOE_CAP = -(-T_ALL // MOE_BM) * MOE_BM
MOE_CAP_BLKS = MOE_CAP // MOE_BM
MOE_SLOTS = N_EXPERTS * MOE_CAP
HALF = D_MODEL // 2
U32 = jnp.uint32


def _pack_rows(x):
    a = lax.bitcast_convert_type(x[:, :HALF].astype(BF16).astype(F32), U32)
    b = lax.bitcast_convert_type(x[:, HALF:].astype(BF16).astype(F32), U32)
    return a | (b >> 16)


def _unpack_rows(w):
    a = lax.bitcast_convert_type(w & U32(0xFFFF0000), F32)
    b = lax.bitcast_convert_type(w << 16, F32)
    return a, b


ROW_TILE = HALF // LANES


def _store_row_tiles(ref, packed):
    n = packed.shape[0]
    for s in range(ROW_TILE):
        ref[pl.ds(s, n, stride=ROW_TILE), :] = packed[:, s * LANES:(s + 1) * LANES]


def _load_row_tiles(ref, n):
    return jnp.concatenate([ref[pl.ds(s, n, stride=ROW_TILE), :] for s in range(ROW_TILE)], axis=1)


def _tile_rows(i):
    return pl.ds(pl.multiple_of(i * ROW_TILE, ROW_TILE), ROW_TILE)


def _shared_ffn(h, wg, wu, wd):
    hb = h.astype(BF16)
    return _dot((_silu(_dot(hb, wg)) * _dot(hb, wu)).astype(BF16), wd)


def _outproj_kernel(ya_ref, yb_ref, yc_ref, yd_ref, w_ref, x_ref, g1_ref, sc2_ref, sh2_ref, lg_ref, lb_ref,
                    rw_ref, rb_ref, swg_ref, swu_ref, swd_ref, cnt_in_ref, *refs, aliased):
    if aliased:
        refs = refs[1:]
    x1_ref, sh_ref, gate_ref, dest_ref, cnt_ref, xs_ref, xpk, dest_v, dest_s, cnt_scr, sem_rows, sem_d = refs
    i = pl.program_id(0)
    last = pl.num_programs(0) - 1
    tm = x_ref.shape[0]

    @pl.when(i == 0)
    def _():
        cnt_scr[...] = cnt_in_ref[...]

    mix = (_dot(ya_ref[...], w_ref[pl.ds(0, W_GROUP), :]) + _dot(yb_ref[...], w_ref[pl.ds(W_GROUP, W_GROUP), :])
           + _dot(yc_ref[...], w_ref[pl.ds(2 * W_GROUP, W_GROUP), :]) + _dot(yd_ref[...], w_ref[pl.ds(3 * W_GROUP, W_GROUP), :]))
    x1 = _layer_norm(ALPHA * x_ref[...] + g1_ref[...] * mix, lg_ref[...], lb_ref[...])
    x1_ref[...] = x1
    h2 = x1 * (1.0 + sc2_ref[...]) + sh2_ref[...]

    sh_ref[...] = _shared_ffn(h2, swg_ref[...], swu_ref[...], swd_ref[...])

    scores = _sigmoid(_dot(h2.astype(BF16), rw_ref[...]))
    sel = scores + rb_ref[...]
    lane_e = lax.broadcasted_iota(jnp.int32, (tm, N_EXPERTS), 1).astype(F32)
    lane_o = lax.broadcasted_iota(jnp.int32, (tm, LANES), 1)
    gate_acc = jnp.zeros((tm, LANES), F32)
    gsum = jnp.zeros((tm, 1), F32)
    chosen = jnp.zeros((tm, N_EXPERTS), F32)
    picks = []
    for k in range(TOP_K):
        best = jnp.max(sel, axis=1, keepdims=True)
        ik = jnp.min(jnp.where(sel == best, lane_e, float(N_EXPERTS)), axis=1, keepdims=True)
        hit = lane_e == ik
        gk = jnp.sum(jnp.where(hit, scores, 0.0), axis=1, keepdims=True)
        sel = jnp.where(hit, -jnp.inf, sel)
        gate_acc = jnp.where(lane_o == k, gk, gate_acc)
        gsum = gsum + gk
        chosen = jnp.where(hit, 1.0, chosen)
        picks.append((ik, hit))
    gate_ref[...] = gate_acc / gsum * ROUTE_SCALE

    r = lax.broadcasted_iota(jnp.int32, (tm, tm), 0)
    c = lax.broadcasted_iota(jnp.int32, (tm, tm), 1)
    earlier = jnp.where(c < r, 1.0, 0.0).astype(BF16)
    rank = cnt_scr[...] + _dot(earlier, chosen.astype(BF16))
    cnt_new = cnt_scr[...] + jnp.sum(chosen, axis=0, keepdims=True)
    cnt_scr[...] = cnt_new
    cnt_ref[...] = cnt_new
    dest_acc = jnp.zeros((tm, LANES), F32)
    for k, (ik, hit) in enumerate(picks):
        pos = jnp.sum(jnp.where(hit, rank, 0.0), axis=1, keepdims=True)
        dest_acc = jnp.where(lane_o == k, ik * float(MOE_CAP) + pos, dest_acc)
    dest_i = dest_acc.astype(jnp.int32)
    dest_ref[...] = dest_i
    dest_v[...] = dest_i
    to_smem = pltpu.make_async_copy(dest_v, dest_s, sem_d)
    to_smem.start()
    to_smem.wait()

    slot = lax.rem(i, 2)

    def wait_rows(s):
        for _ in range(TOP_K):
            pltpu.make_async_copy(xpk.at[s], xs_ref.at[pl.ds(0, tm * ROW_TILE)], sem_rows.at[s]).wait()

    @pl.when(i >= 2)
    def _():
        wait_rows(slot)

    _store_row_tiles(xpk.at[slot], _pack_rows(h2))

    def issue(t, carry):
        for k in range(TOP_K):
            pltpu.make_async_copy(xpk.at[slot, _tile_rows(t)], xs_ref.at[_tile_rows(dest_s[t, k])],
                                  sem_rows.at[slot]).start(priority=k % 2)
        return carry

    lax.fori_loop(0, tm, issue, 0)

    @pl.when(i == last)
    def _():
        wait_rows(slot)

        @pl.when(i >= 1)
        def _():
            wait_rows(1 - slot)


def _outproj_call(ya, yb, yc, yd, w_out_bf, x, ada, prm, sh_w, cnt_in, layer, rows_per_b, tm, xs_prev=None):
    m = x.shape[0]
    r = ada.shape[1]
    aliased = xs_prev is not None

    def full(arr):
        nd = arr.ndim
        return pl.BlockSpec(arr.shape, lambda i: (0,) * nd)

    def resident(shape):
        return pl.BlockSpec((None,) + shape, lambda i: (layer, 0, 0), pipeline_mode=pl.Buffered(1))

    grp = pl.BlockSpec((tm, W_GROUP), lambda i: (i, 0))
    row = pl.BlockSpec((tm, D_MODEL), lambda i: (i, 0))
    lane_out = pl.BlockSpec((tm, LANES), lambda i: (i, 0))
    cnt_spec = pl.BlockSpec((1, N_EXPERTS), lambda i: (0, 0))

    def mod(colidx):
        return pl.BlockSpec((None, r, D_MODEL), _mod_spec(rows_per_b, tm, colidx))

    small = [prm["ln1_g"], prm["ln1_b"], prm["router_w_bf"], prm["router_bias"]]
    in_specs = ([grp, grp, grp, grp, resident((D_MODEL, D_MODEL)), row, mod(2), mod(4), mod(3)]
                + [full(a) for a in small]
                + [resident((D_MODEL, D_EXPERT)), resident((D_MODEL, D_EXPERT)), resident((D_EXPERT, D_MODEL)), cnt_spec])
    args = [ya, yb, yc, yd, w_out_bf, x, ada, ada, ada, *small, *sh_w, cnt_in]
    if aliased:
        in_specs.append(pl.BlockSpec(memory_space=pl.ANY))
        args.append(xs_prev)
    return pl.pallas_call(
        functools.partial(_outproj_kernel, aliased=aliased),
        out_shape=(jax.ShapeDtypeStruct((m, D_MODEL), F32), jax.ShapeDtypeStruct((m, D_MODEL), F32),
                   jax.ShapeDtypeStruct((m, LANES), F32), jax.ShapeDtypeStruct((m, LANES), jnp.int32),
                   jax.ShapeDtypeStruct((1, N_EXPERTS), F32), jax.ShapeDtypeStruct((MOE_SLOTS * ROW_TILE, LANES), U32)),
        grid=(m // tm,),
        in_specs=in_specs,
        out_specs=(row, row, lane_out, lane_out, cnt_spec, pl.BlockSpec(memory_space=pl.ANY)),
        scratch_shapes=[pltpu.VMEM((2, tm * ROW_TILE, LANES), U32), pltpu.VMEM((tm, LANES), jnp.int32),
                        pltpu.SMEM((tm, LANES), jnp.int32), pltpu.VMEM((1, N_EXPERTS), F32),
                        pltpu.SemaphoreType.DMA((2,)), pltpu.SemaphoreType.DMA],
        input_output_aliases={len(args) - 1: 5} if aliased else {},
        compiler_params=_cparams(("arbitrary",)),
        name="outproj_route_dispatch",
    )(*args)


def _expert_plan(counts):
    experts = jnp.arange(N_EXPERTS, dtype=jnp.int32)
    nblk = (counts + MOE_BM - 1) // MOE_BM
    run_end = jnp.cumsum(nblk)
    run_start = run_end - nblk
    bi = jnp.arange(MOE_NB, dtype=jnp.int32)
    e_of = jnp.minimum(jnp.sum((run_end[None, :] <= bi[:, None]).astype(jnp.int32), axis=1), N_EXPERTS - 1)
    onehot = (e_of[:, None] == experts[None, :]).astype(jnp.int32)
    pick = lambda v: jnp.sum(onehot * v[None, :], axis=1)
    b_in = bi - pick(run_start)
    nonempty = nblk > 0
    later = nonempty[None, :] & (experts[None, :] > experts[:, None])
    nxt = jnp.min(jnp.where(later, experts[None, :], N_EXPERTS), axis=1)
    nxt = jnp.where(nxt == N_EXPERTS, -1, nxt)
    run_rank = jnp.cumsum(nonempty.astype(jnp.int32)) - 1
    i32 = lambda v: v.astype(jnp.int32)
    return (i32(e_of), i32(e_of * MOE_CAP_BLKS + b_in), i32(jnp.clip(pick(counts) - b_in * MOE_BM, 0, MOE_BM)),
            i32(b_in == 0), i32(pick(run_rank) & 1), i32(pick(nxt)), i32(run_end[-1:]))


def _expert_kernel(be_ref, bo_ref, nv_ref, bf_ref, bs_ref, bn_ref, nu_ref, x_ref, wg_hbm, wu_hbm, wd_hbm, o_ref,
                   wg_f, wu_f, wd_f, wg_s, wu_s, wd_s, sem, *, layer):
    i = pl.program_id(0)

    def weight_copies(e, slot):
        return (pltpu.make_async_copy(wg_hbm.at[layer, e], wg_f.at[slot], sem.at[slot]),
                pltpu.make_async_copy(wu_hbm.at[layer, e], wu_f.at[slot], sem.at[slot]),
                pltpu.make_async_copy(wd_hbm.at[layer, e], wd_f.at[slot], sem.at[slot]))

    @pl.when(i == 0)
    def _():
        for cp in weight_copies(be_ref[0], 0):
            cp.start()

    @pl.when(i < nu_ref[0])
    def _():
        @pl.when(bf_ref[i] == 1)
        def _():
            slot = bs_ref[i]
            for cp in weight_copies(be_ref[i], slot):
                cp.wait()
            nxt = bn_ref[i]

            @pl.when(nxt >= 0)
            def _():
                for cp in weight_copies(nxt, 1 - slot):
                    cp.start()

            wg_s[...] = wg_f[slot].astype(BF16)
            wu_s[...] = wu_f[slot].astype(BF16)
            wd_s[...] = wd_f[slot].astype(BF16)

        rows = lax.broadcasted_iota(jnp.int32, (MOE_BM, 1), 0)
        a, b = _unpack_rows(jnp.where(rows < nv_ref[i], _load_row_tiles(x_ref, MOE_BM), U32(0)))
        x = jnp.concatenate([a.astype(BF16), b.astype(BF16)], axis=1)
        g = _dot(x, wg_s[...])
        u = _dot(x, wu_s[...])
        _store_row_tiles(o_ref, _pack_rows(_dot((_silu(g) * u).astype(BF16), wd_s[...])))


def _expert_call(plan, xs, w_gate, w_up, w_down, layer):
    def rows_map(i, be, bo, nv, bf, bs, bn, nu):
        return (bo[jnp.minimum(i, nu[0] - 1)], 0)

    rows = pl.BlockSpec((MOE_BM * ROW_TILE, LANES), rows_map)
    hbm = pl.BlockSpec(memory_space=pl.ANY)
    grid_spec = pltpu.PrefetchScalarGridSpec(
        num_scalar_prefetch=7,
        grid=(MOE_NB,),
        in_specs=[rows, hbm, hbm, hbm],
        out_specs=rows,
        scratch_shapes=[pltpu.VMEM((2, D_MODEL, D_EXPERT), F32), pltpu.VMEM((2, D_MODEL, D_EXPERT), F32),
                        pltpu.VMEM((2, D_EXPERT, D_MODEL), F32),
                        pltpu.VMEM((D_MODEL, D_EXPERT), BF16), pltpu.VMEM((D_MODEL, D_EXPERT), BF16),
                        pltpu.VMEM((D_EXPERT, D_MODEL), BF16), pltpu.SemaphoreType.DMA((2,))],
    )
    return pl.pallas_call(
        functools.partial(_expert_kernel, layer=layer),
        out_shape=jax.ShapeDtypeStruct((MOE_SLOTS * ROW_TILE, LANES), U32),
        grid_spec=grid_spec,
        compiler_params=_cparams(("arbitrary",)),
        name="moe_experts",
    )(*plan, xs, w_gate, w_up, w_down)


def _combine_kernel(dest_ref, rows_ref, gate_ref, sh_ref, x1_ref, g2_ref, lg_ref, lb_ref, o_ref, gbuf, sem):
    tm = x1_ref.shape[0]
    i = pl.program_id(0)
    slot = lax.rem(i, 2)

    def issue(step, s):
        base = step * tm

        def body(t, carry):
            for k in range(TOP_K):
                d = dest_ref[(base + t) * TOP_K + k]
                pltpu.make_async_copy(rows_ref.at[_tile_rows(d)], gbuf.at[s, k, _tile_rows(t)],
                                      sem.at[s]).start(priority=k % 2)
            return carry

        lax.fori_loop(0, tm, body, 0)

    @pl.when(i == 0)
    def _():
        issue(0, 0)

    @pl.when(i + 1 < pl.num_programs(0))
    def _():
        issue(i + 1, 1 - slot)

    for k in range(TOP_K):
        pltpu.make_async_copy(rows_ref.at[pl.ds(0, tm * ROW_TILE)], gbuf.at[slot, k], sem.at[slot]).wait()
    gate = gate_ref[...]
    sh = sh_ref[...]
    lo = sh[:, :HALF]
    hi = sh[:, HALF:]
    for k in range(TOP_K):
        a, b = _unpack_rows(_load_row_tiles(gbuf.at[slot, k], tm))
        gk = gate[:, k:k + 1]
        lo = lo + gk * a
        hi = hi + gk * b
    moe = jnp.concatenate([lo, hi], axis=1)
    o_ref[...] = _layer_norm(ALPHA * x1_ref[...] + g2_ref[...] * moe, lg_ref[...], lb_ref[...])


def _combine_call(dest, out_rows, gate, sh, x1, ada, prm, rows_per_b, tm):
    m = x1.shape[0]
    r = ada.shape[1]
    row = pl.BlockSpec((tm, D_MODEL), lambda i, d: (i, 0))
    vec = pl.BlockSpec((1, D_MODEL), lambda i, d: (0, 0))
    grid_spec = pltpu.PrefetchScalarGridSpec(
        num_scalar_prefetch=1,
        grid=(m // tm,),
        in_specs=[pl.BlockSpec(memory_space=pl.ANY),
                  pl.BlockSpec((tm, LANES), lambda i, d: (i, 0)), row, row,
                  pl.BlockSpec((None, r, D_MODEL), _mod_spec(rows_per_b, tm, 5)), vec, vec],
        out_specs=row,
        scratch_shapes=[pltpu.VMEM((2, TOP_K, tm * ROW_TILE, LANES), U32), pltpu.SemaphoreType.DMA((2,))],
    )
    return pl.pallas_call(
        _combine_kernel,
        out_shape=jax.ShapeDtypeStruct((m, D_MODEL), F32),
        grid_spec=grid_spec,
        compiler_params=_cparams(("arbitrary",)),
        name="moe_combine",
    )(dest, out_rows, gate, sh, x1, ada, prm["ln2_g"], prm["ln2_b"])


def _block_diag(w):
    eye = jnp.eye(LRU_BLOCKS, dtype=w.dtype)
    full = jnp.einsum("hcd,hg->hcgd", w, eye).reshape(W_GROUP, W_GROUP)
    return full.astype(BF16)


def _rope_tables(pos):
    half = HEAD_DIM // 2
    freqs = ROPE_THETA ** (-jnp.arange(half, dtype=F32) / half)
    ang = pos.astype(F32)[:, None] * freqs[None, :]
    cos = jnp.cos(ang)
    sin = jnp.sin(ang)
    cos_t = jnp.concatenate([cos, cos, cos, cos], axis=1)
    sin_t = jnp.concatenate([-sin, sin, -sin, sin], axis=1)
    return cos_t, sin_t


def kernel(x_prompt, x_sample, cache_k, cache_v, state_conv_a, state_lru, state_conv_c, state_conv_d, page_table, c_prompt, c_sample, w_ada, b_ada, w_in, conv_a_w, conv_a_b, lru_wr, lru_br, lru_wi, lru_bi, lru_lambda, lam_q1, lam_k1, lam_q2, lam_k2, subln_g, conv_c_w, conv_c_b, ln_c_g, ln_c_b, conv_d_w, w_out, ln1_g, ln1_b, ln2_g, ln2_b, router_w, router_bias, exp_w_gate, exp_w_up, exp_w_down, sh_w_gate, sh_w_up, sh_w_down):
    n_pool = cache_k.shape[1]
    c_all = jnp.concatenate([c_prompt, c_sample, jnp.zeros((16 - BATCH - DEC_BATCH, D_MODEL), F32)], axis=0)
    ada_all = _ada_call(c_all, w_ada, b_ada)
    w_in_bf = w_in.astype(BF16)
    w_out_bf = w_out.astype(BF16)
    sh_wg_bf = sh_w_gate.astype(BF16)
    sh_wu_bf = sh_w_up.astype(BF16)
    sh_wd_bf = sh_w_down.astype(BF16)
    sh_w = (sh_wg_bf, sh_wu_bf, sh_wd_bf)
    cnt0 = jnp.zeros((1, N_EXPERTS), F32)
    cache_k4 = cache_k.reshape(DEPTH, n_pool, PAGE_SIZE * N_HEADS, LANES)
    cache_v4 = cache_v.reshape(DEPTH, n_pool, PAGE_SIZE * N_HEADS, LANES)
    cos_p, sin_p = _rope_tables(jnp.arange(SEQ, dtype=jnp.int32))
    cos_s, sin_s = _rope_tables(jnp.full((1,), PAST_LEN, jnp.int32))
    lane = jnp.arange(LANES)
    comp_mask = jnp.stack([lane < HEAD_DIM, lane >= HEAD_DIM]).astype(F32)

    xp = x_prompt.reshape(T_PROMPT, D_MODEL)
    xs = x_sample.reshape(DEC_BATCH, D_MODEL)
    outs = [[] for _ in range(12)]
    for l in range(DEPTH):
        lam_init = 0.8 - 0.6 * math.exp(-0.3 * l)
        prm = dict(
            conv_a_w=conv_a_w[l], conv_a_b=conv_a_b[l][None], wr=_block_diag(lru_wr[l]), br=lru_br[l][None],
            wi=_block_diag(lru_wi[l]), bi=lru_bi[l][None], lam=lru_lambda[l][None],
            conv_c_w=conv_c_w[l], conv_c_b=conv_c_b[l][None], ln_c_g=ln_c_g[l][None], ln_c_b=ln_c_b[l][None],
            conv_d_w=conv_d_w[l],
            lam_q1=lam_q1[l][None], lam_k1=lam_k1[l][None], lam_q2=lam_q2[l][None], lam_k2=lam_k2[l][None],
            subln_g=subln_g[l][None], subln_g_col=subln_g[l][:, None], lam_init=jnp.full((1, 1), lam_init, F32),
            ln1_g=ln1_g[l][None], ln1_b=ln1_b[l][None], ln2_g=ln2_g[l][None], ln2_b=ln2_b[l][None],
            router_w_bf=router_w[l].astype(BF16), router_bias=router_bias[l][None],
        )
        ada_p = ada_all[l, :BATCH].reshape(BATCH, 1, 6 * D_MODEL)
        ada_s = ada_all[l, BATCH:BATCH + DEC_BATCH].reshape(1, DEC_BATCH, 6 * D_MODEL)

        proj_p = _inproj_call(xp, ada_p, w_in_bf, l, SEQ, 1024, 1024).reshape(BATCH, SEQ, D_IN)
        ya, yc, yd, q_p, k_p, kh_p, vh_p, sa, sh, sc, sd = _mixer_prompt_call(proj_p, cos_p, sin_p, prm)
        yb = _flash_call(q_p, k_p, proj_p, prm)
        outs[0].append(kh_p)
        outs[1].append(vh_p)
        outs[2].append(sa)
        outs[3].append(sh.reshape(BATCH, W_GROUP))
        outs[4].append(sc)
        outs[5].append(sd)

        proj_s = _inproj_call(xs, ada_s, w_in_bf, l, DEC_BATCH, DEC_BATCH, 512)
        ya_s, yc_s, yd_s, q_s, k_s, sa_s, sh_s, sc_s, sd_s = _mixer_sample_call(
            proj_s, cos_s, sin_s, prm,
            state_conv_a[l].transpose(1, 0, 2), state_lru[l], state_conv_c[l].transpose(1, 0, 2),
            state_conv_d[l].transpose(1, 0, 2))
        v_s = proj_s[:, 4 * W_GROUP:5 * W_GROUP]
        q4 = q_s.reshape(DEC_BATCH, 1, N_HEADS, LANES) * (HEAD_DIM ** -0.5)
        q8 = (q4 * comp_mask[None, :, None, :]).reshape(DEC_BATCH, 2 * N_HEADS, LANES).astype(BF16)
        pad4 = jnp.zeros((DEC_BATCH, N_HEADS, LANES), F32)
        k_new8 = jnp.concatenate([k_s.reshape(DEC_BATCH, N_HEADS, LANES), pad4], axis=1)
        v_new8 = jnp.concatenate([v_s.reshape(DEC_BATCH, N_HEADS, LANES), pad4], axis=1)
        yb_s = _decode_call(q8, k_new8, v_new8, cache_k4, cache_v4, page_table, prm, l)
        yb_s = yb_s.reshape(DEC_BATCH, W_GROUP).astype(BF16)
        outs[6].append(k_s.reshape(DEC_BATCH, 1, N_HEADS, 2 * HEAD_DIM))
        outs[7].append(v_s.reshape(DEC_BATCH, 1, N_HEADS, 2 * HEAD_DIM))
        outs[8].append(sa_s.transpose(1, 0, 2))
        outs[9].append(sh_s)
        outs[10].append(sc_s.transpose(1, 0, 2))
        outs[11].append(sd_s.transpose(1, 0, 2))

        flat = lambda t: t.reshape(T_PROMPT, W_GROUP)
        x1_p, shared_p, gate_p, dest_p, cnt_p, x_sorted = _outproj_call(
            flat(ya), flat(yb), flat(yc), flat(yd), w_out_bf, xp, ada_p, prm, sh_w, cnt0, l, SEQ, 256)
        x1_s, shared_s, gate_s, dest_s, cnt_all, x_sorted = _outproj_call(
            ya_s, yb_s, yc_s, yd_s, w_out_bf, xs, ada_s, prm, sh_w, cnt_p, l, DEC_BATCH, DEC_BATCH, xs_prev=x_sorted)

        plan = _expert_plan(cnt_all[0].astype(jnp.int32))
        out_rows = _expert_call(plan, x_sorted, exp_w_gate, exp_w_up, exp_w_down, l)
        xp = _combine_call(dest_p[:, :TOP_K].reshape(-1), out_rows, gate_p, shared_p, x1_p, ada_p, prm, SEQ, 128)
        xs = _combine_call(dest_s[:, :TOP_K].reshape(-1), out_rows, gate_s, shared_s, x1_s, ada_s, prm, DEC_BATCH, DEC_BATCH)

    st = [jnp.stack(o, axis=0) for o in outs]
    kv_shape = (DEPTH, BATCH, SEQ // PAGE_SIZE, PAGE_SIZE, N_HEADS, 2 * HEAD_DIM)
    return (xp.reshape(BATCH, SEQ, D_MODEL), xs.reshape(DEC_BATCH, 1, D_MODEL),
            st[0].reshape(kv_shape), st[1].reshape(kv_shape), st[2], st[3], st[4], st[5],
            st[6], st[7], st[8], st[9], st[10], st[11])
```

```python
import functools
import math

import jax
import jax.numpy as jnp
from jax import lax
from jax.experimental import pallas as pl
from jax.experimental.pallas import tpu as pltpu

F32 = jnp.float32
BF16 = jnp.bfloat16

D_MODEL = 2048
BATCH = 2
SEQ = 4096
DEPTH = 4
DEC_BATCH = 8
PAST_LEN = 16384
PAGE_SIZE = 128
W_GROUP = 512
N_IN_PARTS = 10
D_IN = N_IN_PARTS * W_GROUP
CONV_A = 4
LRU_BLOCKS = 8
LRU_BLOCK_DIM = W_GROUP // LRU_BLOCKS
LRU_C = 8.0
HEAD_DIM = 64
N_HEADS = 4
ROPE_THETA = 10000.0
CONV_C = 31
CONV_D = 3
N_EXPERTS = 64
TOP_K = 8
D_EXPERT = 512
ROUTE_SCALE = 2.5
LN_EPS = 1e-5
ALPHA = (2.0 * DEPTH) ** 0.25
NEG_INF = -1e30

LANES = 128
VMEM_LIMIT = 56 * 1024 * 1024

T_PROMPT = BATCH * SEQ
T_ALL = T_PROMPT + DEC_BATCH
MOE_BM = 256
MOE_NB = -(-T_ALL * TOP_K // MOE_BM) + N_EXPERTS
N_PAGES = PAST_LEN // PAGE_SIZE
DEC_G = 16


def _cparams(sem, **kw):
    return pltpu.CompilerParams(dimension_semantics=sem, vmem_limit_bytes=VMEM_LIMIT, **kw)


def _dot(a, b):
    return jnp.dot(a, b, preferred_element_type=F32)


def _dot_nt(a, b):
    return lax.dot_general(a, b, (((1,), (1,)), ((), ())), preferred_element_type=F32)


def _sigmoid(x):
    return 1.0 / (1.0 + jnp.exp(-x))


def _silu(x):
    return x * _sigmoid(x)


def _layer_norm(z, g, b):
    mu = jnp.mean(z, axis=-1, keepdims=True)
    zc = z - mu
    var = jnp.mean(zc * zc, axis=-1, keepdims=True)
    return zc * lax.rsqrt(var + LN_EPS) * g + b


def _ada_kernel(c_ref, w_ref, b_ref, o_ref):
    c = c_ref[...]
    o_ref[...] = _dot(_silu(c).astype(BF16), w_ref[...].astype(BF16)) + b_ref[...]


def _ada_call(c_all, w_ada, b_ada):
    r = c_all.shape[0]
    tn = 1024
    return pl.pallas_call(
        _ada_kernel,
        out_shape=jax.ShapeDtypeStruct((DEPTH, r, 6 * D_MODEL), F32),
        grid=(DEPTH, 6 * D_MODEL // tn),
        in_specs=[
            pl.BlockSpec((r, D_MODEL), lambda l, n: (0, 0)),
            pl.BlockSpec((None, D_MODEL, tn), lambda l, n: (l, 0, n)),
            pl.BlockSpec((None, 1, tn), lambda l, n: (l, 0, n)),
        ],
        out_specs=pl.BlockSpec((None, r, tn), lambda l, n: (l, 0, n)),
        compiler_params=_cparams(("arbitrary", "arbitrary")),
        name="ada",
    )(c_all, w_ada, b_ada.reshape(DEPTH, 1, 6 * D_MODEL))


def _inproj_kernel(x_ref, sc_ref, sh_ref, w_ref, o_ref, h_scr):
    @pl.when(pl.program_id(1) == 0)
    def _():
        h_scr[...] = (x_ref[...] * (1.0 + sc_ref[...]) + sh_ref[...]).astype(BF16)

    o_ref[...] = _dot(h_scr[...], w_ref[...])


def _mod_spec(rows_per_b, tm, col):
    def imap(m, *_):
        return ((m * tm) // rows_per_b, 0, col)
    return imap


def _inproj_call(x, ada, w_in_bf, layer, rows_per_b, tm, tn):
    m = x.shape[0]
    r = ada.shape[1]
    return pl.pallas_call(
        _inproj_kernel,
        out_shape=jax.ShapeDtypeStruct((m, D_IN), F32),
        grid=(m // tm, D_IN // tn),
        in_specs=[
            pl.BlockSpec((tm, D_MODEL), lambda i, n: (i, 0)),
            pl.BlockSpec((None, r, D_MODEL), _mod_spec(rows_per_b, tm, 1)),
            pl.BlockSpec((None, r, D_MODEL), _mod_spec(rows_per_b, tm, 0)),
            pl.BlockSpec((None, D_MODEL, tn), lambda i, n: (layer, 0, n)),
        ],
        out_specs=pl.BlockSpec((tm, tn), lambda i, n: (i, n)),
        scratch_shapes=[pltpu.VMEM((tm, D_MODEL), BF16)],
        compiler_params=_cparams(("arbitrary", "arbitrary")),
        name="inproj",
    )(x, ada, ada, w_in_bf)


def _softplus(z):
    return jnp.maximum(z, 0.0) + jnp.log1p(jnp.exp(-jnp.abs(z)))


def _gelu_tanh(x):
    return 0.5 * x * (1.0 + jnp.tanh(math.sqrt(2.0 / math.pi) * (x + 0.044715 * (x * x * x))))


def _lru_gates(xa, wr, br, wi, bi, lam):
    xb = xa.astype(BF16)
    r = _sigmoid(_dot(xb, wr) + br)
    i = _sigmoid(_dot(xb, wi) + bi)
    log_a = -LRU_C * r * _softplus(-lam)
    a = jnp.exp(log_a)
    u = jnp.sqrt(jnp.tanh(-log_a) * (1.0 + a * a)) * (i * xa)
    return a, u


def _rope(x, cos_t, sin_t):
    lane = lax.broadcasted_iota(jnp.int32, x.shape, 1)
    first_half = (lane & (HEAD_DIM // 2)) == 0
    w = x.shape[1]
    partner = jnp.where(first_half, pltpu.roll(x, w - HEAD_DIM // 2, axis=1), pltpu.roll(x, HEAD_DIM // 2, axis=1))
    return x * cos_t + partner * sin_t


def _tile4(t):
    return jnp.concatenate([t, t, t, t], axis=1)


def _conv_from_buf(buf_ref, w_ref, hist, n_taps, rows, rb):
    outs = []
    for r0 in range(0, rows, rb):
        acc = None
        for j in range(n_taps):
            term = buf_ref[pl.ds(hist - (n_taps - 1) + j + r0, rb), :] * w_ref[pl.ds(j, 1), :]
            acc = term if acc is None else acc + term
        outs.append(acc)
    return jnp.concatenate(outs, axis=0) if len(outs) > 1 else outs[0]


def _linear_scan(a, b):
    n = a.shape[0]
    rows = lax.broadcasted_iota(jnp.int32, a.shape, 0)
    d = 1
    while d < n:
        keep = rows >= d
        a_sh = jnp.where(keep, pltpu.roll(a, d, axis=0), 1.0)
        b_sh = jnp.where(keep, pltpu.roll(b, d, axis=0), 0.0)
        b = a * b_sh + b
        a = a * a_sh
        d *= 2
    return a, b


MIX_TC = 256
HIST_A = 8
HIST_C = 32
HIST_D = 8
CONV_RB = 32


def _store_head_rows(ref, x):
    n = x.shape[0]
    for h in range(N_HEADS):
        ref[pl.ds(h, n, stride=N_HEADS), :] = x[:, h * LANES:(h + 1) * LANES]


def _mixer_prompt_kernel(ax_ref, ag_ref, bq_ref, bk_ref, bv_ref, cv_ref, cg_ref, db_ref, dc_ref, dx_ref,
                         cos_ref, sin_ref, caw_ref, cab_ref, wr_ref, br_ref, wi_ref, bi_ref, lam_ref,
                         ccw_ref, ccb_ref, lng_ref, lnb_ref, cdw_ref,
                         ya_ref, yc_ref, yd_ref, q_ref, k_ref, kh_ref, vh_ref, sa_ref, sh_ref, sc_ref, sd_ref,
                         xa_buf, uc_buf, dd_buf, h_buf):
    tc = MIX_TC

    @pl.when(pl.program_id(1) == 0)
    def _():
        xa_buf[pl.ds(0, HIST_A), :] = jnp.zeros((HIST_A, W_GROUP), F32)
        uc_buf[pl.ds(0, HIST_C), :] = jnp.zeros((HIST_C, W_GROUP), F32)
        dd_buf[pl.ds(0, HIST_D), :] = jnp.zeros((HIST_D, W_GROUP), F32)
        h_buf[...] = jnp.zeros((1, W_GROUP), F32)

    xa_buf[pl.ds(HIST_A, tc), :] = ax_ref[...]
    xa = _conv_from_buf(xa_buf, caw_ref, HIST_A, CONV_A, tc, CONV_RB) + cab_ref[...]
    a, u = _lru_gates(xa, wr_ref[...], br_ref[...], wi_ref[...], bi_ref[...], lam_ref[...])
    a_cum, b_cum = _linear_scan(a, u)
    h = a_cum * h_buf[...] + b_cum
    ya_ref[...] = (_gelu_tanh(ag_ref[...]) * h).astype(BF16)
    h_last = h[tc - 1:tc, :]
    h_buf[...] = h_last
    sh_ref[...] = h_last
    sa_ref[...] = xa_buf[pl.ds(HIST_A + tc - (CONV_A - 1), CONV_A - 1), :]
    xa_buf[pl.ds(0, HIST_A), :] = xa_buf[pl.ds(tc, HIST_A), :]

    uc_buf[pl.ds(HIST_C, tc), :] = cv_ref[...] * _sigmoid(cg_ref[...])
    yc = _conv_from_buf(uc_buf, ccw_ref, HIST_C, CONV_C, tc, CONV_RB) + ccb_ref[...]
    yc_ref[...] = _silu(_layer_norm(yc, lng_ref[...], lnb_ref[...])).astype(BF16)
    sc_ref[...] = uc_buf[pl.ds(HIST_C + tc - (CONV_C - 1), CONV_C - 1), :]
    uc_buf[pl.ds(0, HIST_C), :] = uc_buf[pl.ds(tc, HIST_C), :]

    dd_buf[pl.ds(HIST_D, tc), :] = dc_ref[...] * dx_ref[...]
    yd = _conv_from_buf(dd_buf, cdw_ref, HIST_D, CONV_D, tc, CONV_RB)
    yd_ref[...] = (db_ref[...] * yd).astype(BF16)
    sd_ref[...] = dd_buf[pl.ds(HIST_D + tc - (CONV_D - 1), CONV_D - 1), :]
    dd_buf[pl.ds(0, HIST_D), :] = dd_buf[pl.ds(tc, HIST_D), :]

    cos_t = _tile4(cos_ref[...])
    sin_t = _tile4(sin_ref[...])
    q_ref[...] = (_rope(bq_ref[...], cos_t, sin_t) * (HEAD_DIM ** -0.5)).astype(BF16)
    k_rot = _rope(bk_ref[...], cos_t, sin_t)
    k_ref[...] = k_rot
    _store_head_rows(kh_ref, k_rot)
    _store_head_rows(vh_ref, bv_ref[...])


def _mixer_prompt_call(proj, cos_t, sin_t, prm):
    tc = MIX_TC

    def col(j):
        return pl.BlockSpec((None, tc, W_GROUP), lambda b, c: (b, c, j))

    def full(arr):
        nd = arr.ndim
        return pl.BlockSpec(arr.shape, lambda b, c: (0,) * nd)

    small = [prm["conv_a_w"], prm["conv_a_b"], prm["wr"], prm["br"], prm["wi"], prm["bi"], prm["lam"],
             prm["conv_c_w"], prm["conv_c_b"], prm["ln_c_g"], prm["ln_c_b"], prm["conv_d_w"]]
    tok = pl.BlockSpec((None, tc, W_GROUP), lambda b, c: (b, c, 0))

    def state(n):
        return pl.BlockSpec((None, n, W_GROUP), lambda b, c: (b, 0, 0))

    out_shape = (
        jax.ShapeDtypeStruct((BATCH, SEQ, W_GROUP), BF16),
        jax.ShapeDtypeStruct((BATCH, SEQ, W_GROUP), BF16),
        jax.ShapeDtypeStruct((BATCH, SEQ, W_GROUP), BF16),
        jax.ShapeDtypeStruct((BATCH, SEQ, W_GROUP), BF16),
        jax.ShapeDtypeStruct((BATCH, SEQ, W_GROUP), F32),
        jax.ShapeDtypeStruct((BATCH, SEQ * N_HEADS, LANES), F32),
        jax.ShapeDtypeStruct((BATCH, SEQ * N_HEADS, LANES), F32),
        jax.ShapeDtypeStruct((BATCH, CONV_A - 1, W_GROUP), F32),
        jax.ShapeDtypeStruct((BATCH, 1, W_GROUP), F32),
        jax.ShapeDtypeStruct((BATCH, CONV_C - 1, W_GROUP), F32),
        jax.ShapeDtypeStruct((BATCH, CONV_D - 1, W_GROUP), F32),
    )
    heads = pl.BlockSpec((None, tc * N_HEADS, LANES), lambda b, c: (b, c, 0))
    return pl.pallas_call(
        _mixer_prompt_kernel,
        out_shape=out_shape,
        grid=(BATCH, SEQ // tc),
        in_specs=[col(0), col(1), col(2), col(3), col(4), col(5), col(6), col(7), col(8), col(9),
                  pl.BlockSpec((tc, LANES), lambda b, c: (c, 0)),
                  pl.BlockSpec((tc, LANES), lambda b, c: (c, 0))] + [full(a) for a in small],
        out_specs=(tok, tok, tok, tok, tok, heads, heads,
                   state(CONV_A - 1), state(1), state(CONV_C - 1), state(CONV_D - 1)),
        scratch_shapes=[pltpu.VMEM((HIST_A + tc, W_GROUP), F32), pltpu.VMEM((HIST_C + tc, W_GROUP), F32),
                        pltpu.VMEM((HIST_D + tc, W_GROUP), F32), pltpu.VMEM((1, W_GROUP), F32)],
        compiler_params=_cparams(("arbitrary", "arbitrary")),
        name="mixer_prompt",
    )(*([proj] * 10), cos_t, sin_t, *small)


def _mixer_sample_kernel(ax_ref, ag_ref, bq_ref, bk_ref, cv_ref, cg_ref, db_ref, dc_ref, dx_ref,
                         cos_ref, sin_ref, caw_ref, cab_ref, wr_ref, br_ref, wi_ref, bi_ref, lam_ref,
                         ccw_ref, ccb_ref, lng_ref, lnb_ref, cdw_ref,
                         ba_ref, h0_ref, bc_ref, bd_ref,
                         ya_ref, yc_ref, yd_ref, q_ref, k_ref, sa_ref, sh_ref, sc_ref, sd_ref):
    ax = ax_ref[...]
    xa = cab_ref[...] + caw_ref[pl.ds(CONV_A - 1, 1), :] * ax
    for j in range(CONV_A - 1):
        xa = xa + caw_ref[pl.ds(j, 1), :] * ba_ref[j]
    a, u = _lru_gates(xa, wr_ref[...], br_ref[...], wi_ref[...], bi_ref[...], lam_ref[...])
    h = a * h0_ref[...] + u
    ya_ref[...] = (_gelu_tanh(ag_ref[...]) * h).astype(BF16)
    sh_ref[...] = h
    for j in range(CONV_A - 2):
        sa_ref[j] = ba_ref[j + 1]
    sa_ref[CONV_A - 2] = ax

    uc = cv_ref[...] * _sigmoid(cg_ref[...])
    yc = ccb_ref[...] + ccw_ref[pl.ds(CONV_C - 1, 1), :] * uc
    for j in range(CONV_C - 1):
        yc = yc + ccw_ref[pl.ds(j, 1), :] * bc_ref[j]
    yc_ref[...] = _silu(_layer_norm(yc, lng_ref[...], lnb_ref[...])).astype(BF16)
    for j in range(CONV_C - 2):
        sc_ref[j] = bc_ref[j + 1]
    sc_ref[CONV_C - 2] = uc

    dd = dc_ref[...] * dx_ref[...]
    yd = cdw_ref[pl.ds(CONV_D - 1, 1), :] * dd
    for j in range(CONV_D - 1):
        yd = yd + cdw_ref[pl.ds(j, 1), :] * bd_ref[j]
    yd_ref[...] = (db_ref[...] * yd).astype(BF16)
    for j in range(CONV_D - 2):
        sd_ref[j] = bd_ref[j + 1]
    sd_ref[CONV_D - 2] = dd

    cos_t = _tile4(cos_ref[...])
    sin_t = _tile4(sin_ref[...])
    q_ref[...] = _rope(bq_ref[...], cos_t, sin_t)
    k_ref[...] = _rope(bk_ref[...], cos_t, sin_t)


def _mixer_sample_call(proj, cos_t, sin_t, prm, buf_a, h0, buf_c, buf_d):
    nb = DEC_BATCH

    def col(j):
        return pl.BlockSpec((nb, W_GROUP), lambda i: (0, j))

    def full(arr):
        nd = arr.ndim
        return pl.BlockSpec(arr.shape, lambda i: (0,) * nd)

    small = [prm["conv_a_w"], prm["conv_a_b"], prm["wr"], prm["br"], prm["wi"], prm["bi"], prm["lam"],
             prm["conv_c_w"], prm["conv_c_b"], prm["ln_c_g"], prm["ln_c_b"], prm["conv_d_w"]]
    states = [buf_a, h0, buf_c, buf_d]
    tok_bf = jax.ShapeDtypeStruct((nb, W_GROUP), BF16)
    tok_f = jax.ShapeDtypeStruct((nb, W_GROUP), F32)
    out_shape = (tok_bf, tok_bf, tok_bf, tok_f, tok_f,
                 jax.ShapeDtypeStruct(buf_a.shape, F32), tok_f,
                 jax.ShapeDtypeStruct(buf_c.shape, F32), jax.ShapeDtypeStruct(buf_d.shape, F32))
    return pl.pallas_call(
        _mixer_sample_kernel,
        out_shape=out_shape,
        grid=(1,),
        in_specs=[col(0), col(1), col(2), col(3), col(5), col(6), col(7), col(8), col(9),
                  full(cos_t), full(sin_t)] + [full(a) for a in small] + [full(a) for a in states],
        out_specs=tuple(pl.BlockSpec(s.shape, lambda i, nd=len(s.shape): (0,) * nd) for s in out_shape),
        compiler_params=_cparams(("arbitrary",)),
        name="mixer_sample",
    )(*([proj] * 9), cos_t, sin_t, *small, *states)


def _diff_lambda(lq1_ref, lk1_ref, lq2_ref, lk2_ref, lam_init):
    s1 = jnp.sum(lq1_ref[...] * lk1_ref[...], axis=-1, keepdims=True)
    s2 = jnp.sum(lq2_ref[...] * lk2_ref[...], axis=-1, keepdims=True)
    return jnp.exp(s1) - jnp.exp(s2) + lam_init


def _diff_finish(o1, o2, lam, g, lam_init):
    od = o1 - lam * o2
    ms = jnp.mean(od * od, axis=-1, keepdims=True)
    return od * lax.rsqrt(ms + LN_EPS) * g * (1.0 - lam_init)


FLASH_T = 512


def _flash_kernel(q_ref, k_ref, v_ref, lq1_ref, lk1_ref, lq2_ref, lk2_ref, gcol_ref, li_ref, o_ref,
                  kb_scr, vt_scr, m_scr, l_scr, acc_scr):
    t = FLASH_T
    i = pl.program_id(2)

    @pl.when(i == 0)
    def _():
        for c in range(SEQ // t):
            kb_scr[c] = k_ref[pl.ds(c * t, t), :].astype(BF16)
            vt_scr[c] = v_ref[pl.ds(c * t, t), :].T.astype(BF16)

    q = q_ref[...]
    lane = lax.broadcasted_iota(jnp.int32, q.shape, 1)
    zero = jnp.zeros_like(q)
    qq = jnp.concatenate([jnp.where(lane < HEAD_DIM, q, zero), jnp.where(lane >= HEAD_DIM, q, zero)], axis=0)
    m_scr[...] = jnp.full(m_scr.shape, NEG_INF, F32)
    l_scr[...] = jnp.zeros(l_scr.shape, F32)
    acc_scr[...] = jnp.zeros(acc_scr.shape, F32)

    def step(j, masked):
        st = _dot_nt(kb_scr[j], qq)
        if masked:
            r = lax.broadcasted_iota(jnp.int32, st.shape, 0)
            c = lax.broadcasted_iota(jnp.int32, st.shape, 1)
            st = jnp.where(r <= jnp.where(c >= t, c - t, c), st, NEG_INF)
        m_prev = m_scr[...]
        m_new = jnp.maximum(m_prev, jnp.max(st, axis=0, keepdims=True))
        alpha = jnp.exp(m_prev - m_new)
        p = jnp.exp(st - m_new)
        l_scr[...] = alpha * l_scr[...] + jnp.sum(p, axis=0, keepdims=True)
        acc_scr[...] = alpha * acc_scr[...] + _dot(vt_scr[j], p.astype(BF16))
        m_scr[...] = m_new

    def body(j, carry):
        step(j, False)
        return carry

    lax.fori_loop(0, i, body, 0)
    step(i, True)

    o = acc_scr[...] / l_scr[...]
    lam_init = li_ref[...]
    lam = _diff_lambda(lq1_ref, lk1_ref, lq2_ref, lk2_ref, lam_init)
    od = o[:, :t] - lam * o[:, t:]
    ms = jnp.mean(od * od, axis=0, keepdims=True)
    y = od * lax.rsqrt(ms + LN_EPS) * gcol_ref[...] * (1.0 - lam_init)
    o_ref[...] = y.T.astype(BF16)


def _flash_call(q, k, proj, prm):
    t = FLASH_T
    v_col0 = 4 * W_GROUP // LANES

    def full(arr):
        nd = arr.ndim
        return pl.BlockSpec(arr.shape, lambda b, h, i: (0,) * nd)

    small = [prm["lam_q1"], prm["lam_k1"], prm["lam_q2"], prm["lam_k2"], prm["subln_g_col"], prm["lam_init"]]
    return pl.pallas_call(
        _flash_kernel,
        out_shape=jax.ShapeDtypeStruct((BATCH, SEQ, W_GROUP), BF16),
        grid=(BATCH, N_HEADS, SEQ // t),
        in_specs=[pl.BlockSpec((None, t, LANES), lambda b, h, i: (b, i, h)),
                  pl.BlockSpec((None, SEQ, LANES), lambda b, h, i: (b, 0, h)),
                  pl.BlockSpec((None, SEQ, LANES), lambda b, h, i: (b, 0, v_col0 + h))] + [full(a) for a in small],
        out_specs=pl.BlockSpec((None, t, LANES), lambda b, h, i: (b, i, h)),
        scratch_shapes=[pltpu.VMEM((SEQ // t, t, LANES), BF16), pltpu.VMEM((SEQ // t, LANES, t), BF16),
                        pltpu.VMEM((1, 2 * t), F32), pltpu.VMEM((1, 2 * t), F32), pltpu.VMEM((LANES, 2 * t), F32)],
        compiler_params=_cparams(("arbitrary", "arbitrary", "arbitrary")),
        name="flash",
    )(q, k, proj, *small)


def _softmax_update(s, v, m_scr, l_scr, acc_scr):
    m_prev = m_scr[...]
    m_new = jnp.maximum(m_prev, jnp.max(s, axis=1, keepdims=True))
    alpha = jnp.exp(m_prev - m_new)
    p = jnp.exp(s - m_new)
    l_scr[...] = alpha * l_scr[...] + jnp.sum(p, axis=1, keepdims=True)
    acc_scr[...] = alpha * acc_scr[...] + _dot(p.astype(BF16), v)
    m_scr[...] = m_new


def _decode_kernel(pt_ref, q_ref, *refs):
    k_refs = refs[:DEC_G]
    v_refs = refs[DEC_G:2 * DEC_G]
    (kn_ref, vn_ref, lq1_ref, lk1_ref, lq2_ref, lk2_ref, g_ref, li_ref, o_ref, m_scr, l_scr, acc_scr) = refs[2 * DEC_G:]
    s_idx = pl.program_id(1)

    @pl.when(s_idx == 0)
    def _():
        m_scr[...] = jnp.full(m_scr.shape, NEG_INF, F32)
        l_scr[...] = jnp.zeros(l_scr.shape, F32)
        acc_scr[...] = jnp.zeros(acc_scr.shape, F32)

    q = q_ref[...]
    n_kv = PAGE_SIZE * N_HEADS
    row = lax.broadcasted_iota(jnp.int32, (2 * N_HEADS, n_kv), 0)
    col = lax.broadcasted_iota(jnp.int32, (2 * N_HEADS, n_kv), 1)
    same_head = (col & (N_HEADS - 1)) == (row & (N_HEADS - 1))
    s = jnp.concatenate([jnp.where(same_head, _dot_nt(q, k_refs[g][...].astype(BF16)), NEG_INF)
                         for g in range(DEC_G)], axis=1)
    m_prev = m_scr[...]
    m_new = jnp.maximum(m_prev, jnp.max(s, axis=1, keepdims=True))
    alpha = jnp.exp(m_prev - m_new)
    p = jnp.exp(s - m_new)
    l_scr[...] = alpha * l_scr[...] + jnp.sum(p, axis=1, keepdims=True)
    pb = p.astype(BF16)
    pv = _dot(pb[:, :n_kv], v_refs[0][...].astype(BF16))
    for g in range(1, DEC_G):
        pv = pv + _dot(pb[:, g * n_kv:(g + 1) * n_kv], v_refs[g][...].astype(BF16))
    acc_scr[...] = alpha * acc_scr[...] + pv
    m_scr[...] = m_new

    @pl.when(s_idx == pl.num_programs(1) - 1)
    def _():
        kn = kn_ref[...].astype(BF16)
        vn = vn_ref[...].astype(BF16)
        r8 = lax.broadcasted_iota(jnp.int32, (2 * N_HEADS, 2 * N_HEADS), 0)
        c8 = lax.broadcasted_iota(jnp.int32, (2 * N_HEADS, 2 * N_HEADS), 1)
        s = jnp.where(c8 == (r8 & (N_HEADS - 1)), _dot_nt(q, kn), NEG_INF)
        _softmax_update(s, vn, m_scr, l_scr, acc_scr)
        o = acc_scr[...] / l_scr[...]
        lam_init = li_ref[...]
        lam = _diff_lambda(lq1_ref, lk1_ref, lq2_ref, lk2_ref, lam_init)
        o_ref[...] = _diff_finish(o[:N_HEADS], o[N_HEADS:], lam, g_ref[...], lam_init)


def _decode_call(q8, k_new8, v_new8, cache_k4, cache_v4, page_table, prm, layer):
    n_kv = PAGE_SIZE * N_HEADS
    steps = N_PAGES // DEC_G

    def page_spec(g):
        return pl.BlockSpec((None, None, n_kv, LANES), lambda b, s, pt: (layer, pt[b, s * DEC_G + g], 0, 0))

    def full(arr):
        nd = arr.ndim
        return pl.BlockSpec(arr.shape, lambda b, s, pt: (0,) * nd)

    row8 = pl.BlockSpec((None, 2 * N_HEADS, LANES), lambda b, s, pt: (b, 0, 0))
    small = [prm["lam_q1"], prm["lam_k1"], prm["lam_q2"], prm["lam_k2"], prm["subln_g"], prm["lam_init"]]
    grid_spec = pltpu.PrefetchScalarGridSpec(
        num_scalar_prefetch=1,
        grid=(DEC_BATCH, steps),
        in_specs=[row8] + [page_spec(g) for g in range(DEC_G)] + [page_spec(g) for g in range(DEC_G)]
                 + [row8, row8] + [full(a) for a in small],
        out_specs=pl.BlockSpec((None, N_HEADS, LANES), lambda b, s, pt: (b, 0, 0)),
        scratch_shapes=[pltpu.VMEM((2 * N_HEADS, 1), F32), pltpu.VMEM((2 * N_HEADS, 1), F32),
                        pltpu.VMEM((2 * N_HEADS, LANES), F32)],
    )
    return pl.pallas_call(
        _decode_kernel,
        out_shape=jax.ShapeDtypeStruct((DEC_BATCH, N_HEADS, LANES), F32),
        grid_spec=grid_spec,
        compiler_params=_cparams(("arbitrary", "arbitrary")),
        name="decode_attn",
    )(page_table, q8, *([cache_k4] * DEC_G), *([cache_v4] * DEC_G), k_new8, v_new8, *small)


MOE_CAP = -(-T_ALL // MOE_BM) * MOE_BM
MOE_CAP_BLKS = MOE_CAP // MOE_BM
MOE_SLOTS = N_EXPERTS * MOE_CAP
HALF = D_MODEL // 2
U32 = jnp.uint32


def _pack_rows(x):
    a = lax.bitcast_convert_type(x[:, :HALF].astype(BF16).astype(F32), U32)
    b = lax.bitcast_convert_type(x[:, HALF:].astype(BF16).astype(F32), U32)
    return a | (b >> 16)


def _unpack_rows(w):
    a = lax.bitcast_convert_type(w & U32(0xFFFF0000), F32)
    b = lax.bitcast_convert_type(w << 16, F32)
    return a, b


ROW_TILE = HALF // LANES


def _store_row_tiles(ref, packed):
    n = packed.shape[0]
    for s in range(ROW_TILE):
        ref[pl.ds(s, n, stride=ROW_TILE), :] = packed[:, s * LANES:(s + 1) * LANES]


def _load_row_tiles(ref, n):
    return jnp.concatenate([ref[pl.ds(s, n, stride=ROW_TILE), :] for s in range(ROW_TILE)], axis=1)


def _tile_rows(i):
    return pl.ds(pl.multiple_of(i * ROW_TILE, ROW_TILE), ROW_TILE)


def _shared_ffn(h, wg, wu, wd):
    hb = h.astype(BF16)
    return _dot((_silu(_dot(hb, wg)) * _dot(hb, wu)).astype(BF16), wd)


def _outproj_kernel(ya_ref, yb_ref, yc_ref, yd_ref, w_ref, x_ref, g1_ref, sc2_ref, sh2_ref, lg_ref, lb_ref,
                    rw_ref, rb_ref, swg_ref, swu_ref, swd_ref, cnt_in_ref, *refs, aliased):
    if aliased:
        refs = refs[1:]
    x1_ref, sh_ref, gate_ref, dest_ref, cnt_ref, xs_ref, xpk, dest_v, dest_s, cnt_scr, sem_rows, sem_d = refs
    i = pl.program_id(0)
    last = pl.num_programs(0) - 1
    tm = x_ref.shape[0]

    @pl.when(i == 0)
    def _():
        cnt_scr[...] = cnt_in_ref[...]

    mix = (_dot(ya_ref[...], w_ref[pl.ds(0, W_GROUP), :]) + _dot(yb_ref[...], w_ref[pl.ds(W_GROUP, W_GROUP), :])
           + _dot(yc_ref[...], w_ref[pl.ds(2 * W_GROUP, W_GROUP), :]) + _dot(yd_ref[...], w_ref[pl.ds(3 * W_GROUP, W_GROUP), :]))
    x1 = _layer_norm(ALPHA * x_ref[...] + g1_ref[...] * mix, lg_ref[...], lb_ref[...])
    x1_ref[...] = x1
    h2 = x1 * (1.0 + sc2_ref[...]) + sh2_ref[...]

    sh_ref[...] = _shared_ffn(h2, swg_ref[...], swu_ref[...], swd_ref[...])

    scores = _sigmoid(_dot(h2.astype(BF16), rw_ref[...]))
    sel = scores + rb_ref[...]
    lane_e = lax.broadcasted_iota(jnp.int32, (tm, N_EXPERTS), 1).astype(F32)
    lane_o = lax.broadcasted_iota(jnp.int32, (tm, LANES), 1)
    gate_acc = jnp.zeros((tm, LANES), F32)
    gsum = jnp.zeros((tm, 1), F32)
    chosen = jnp.zeros((tm, N_EXPERTS), F32)
    picks = []
    for k in range(TOP_K):
        best = jnp.max(sel, axis=1, keepdims=True)
        ik = jnp.min(jnp.where(sel == best, lane_e, float(N_EXPERTS)), axis=1, keepdims=True)
        hit = lane_e == ik
        gk = jnp.sum(jnp.where(hit, scores, 0.0), axis=1, keepdims=True)
        sel = jnp.where(hit, -jnp.inf, sel)
        gate_acc = jnp.where(lane_o == k, gk, gate_acc)
        gsum = gsum + gk
        chosen = jnp.where(hit, 1.0, chosen)
        picks.append((ik, hit))
    gate_ref[...] = gate_acc / gsum * ROUTE_SCALE

    r = lax.broadcasted_iota(jnp.int32, (tm, tm), 0)
    c = lax.broadcasted_iota(jnp.int32, (tm, tm), 1)
    earlier = jnp.where(c < r, 1.0, 0.0).astype(BF16)
    rank = cnt_scr[...] + _dot(earlier, chosen.astype(BF16))
    cnt_new = cnt_scr[...] + jnp.sum(chosen, axis=0, keepdims=True)
    cnt_scr[...] = cnt_new
    cnt_ref[...] = cnt_new
    dest_acc = jnp.zeros((tm, LANES), F32)
    for k, (ik, hit) in enumerate(picks):
        pos = jnp.sum(jnp.where(hit, rank, 0.0), axis=1, keepdims=True)
        dest_acc = jnp.where(lane_o == k, ik * float(MOE_CAP) + pos, dest_acc)
    dest_i = dest_acc.astype(jnp.int32)
    dest_ref[...] = dest_i
    dest_v[...] = dest_i
    to_smem = pltpu.make_async_copy(dest_v, dest_s, sem_d)
    to_smem.start()
    to_smem.wait()

    slot = lax.rem(i, 2)

    def wait_rows(s):
        for _ in range(TOP_K):
            pltpu.make_async_copy(xpk.at[s], xs_ref.at[pl.ds(0, tm * ROW_TILE)], sem_rows.at[s]).wait()

    @pl.when(i >= 2)
    def _():
        wait_rows(slot)

    _store_row_tiles(xpk.at[slot], _pack_rows(h2))

    def issue(t, carry):
        for k in range(TOP_K):
            pltpu.make_async_copy(xpk.at[slot, _tile_rows(t)], xs_ref.at[_tile_rows(dest_s[t, k])],
                                  sem_rows.at[slot]).start(priority=k % 2)
        return carry

    lax.fori_loop(0, tm, issue, 0)

    @pl.when(i == last)
    def _():
        wait_rows(slot)

        @pl.when(i >= 1)
        def _():
            wait_rows(1 - slot)


def _outproj_call(ya, yb, yc, yd, w_out_bf, x, ada, prm, sh_w, cnt_in, layer, rows_per_b, tm, xs_prev=None):
    m = x.shape[0]
    r = ada.shape[1]
    aliased = xs_prev is not None

    def full(arr):
        nd = arr.ndim
        return pl.BlockSpec(arr.shape, lambda i: (0,) * nd)

    def resident(shape):
        return pl.BlockSpec((None,) + shape, lambda i: (layer, 0, 0), pipeline_mode=pl.Buffered(1))

    grp = pl.BlockSpec((tm, W_GROUP), lambda i: (i, 0))
    row = pl.BlockSpec((tm, D_MODEL), lambda i: (i, 0))
    lane_out = pl.BlockSpec((tm, LANES), lambda i: (i, 0))
    cnt_spec = pl.BlockSpec((1, N_EXPERTS), lambda i: (0, 0))

    def mod(colidx):
        return pl.BlockSpec((None, r, D_MODEL), _mod_spec(rows_per_b, tm, colidx))

    small = [prm["ln1_g"], prm["ln1_b"], prm["router_w_bf"], prm["router_bias"]]
    in_specs = ([grp, grp, grp, grp, resident((D_MODEL, D_MODEL)), row, mod(2), mod(4), mod(3)]
                + [full(a) for a in small]
                + [resident((D_MODEL, D_EXPERT)), resident((D_MODEL, D_EXPERT)), resident((D_EXPERT, D_MODEL)), cnt_spec])
    args = [ya, yb, yc, yd, w_out_bf, x, ada, ada, ada, *small, *sh_w, cnt_in]
    if aliased:
        in_specs.append(pl.BlockSpec(memory_space=pl.ANY))
        args.append(xs_prev)
    return pl.pallas_call(
        functools.partial(_outproj_kernel, aliased=aliased),
        out_shape=(jax.ShapeDtypeStruct((m, D_MODEL), F32), jax.ShapeDtypeStruct((m, D_MODEL), F32),
                   jax.ShapeDtypeStruct((m, LANES), F32), jax.ShapeDtypeStruct((m, LANES), jnp.int32),
                   jax.ShapeDtypeStruct((1, N_EXPERTS), F32), jax.ShapeDtypeStruct((MOE_SLOTS * ROW_TILE, LANES), U32)),
        grid=(m // tm,),
        in_specs=in_specs,
        out_specs=(row, row, lane_out, lane_out, cnt_spec, pl.BlockSpec(memory_space=pl.ANY)),
        scratch_shapes=[pltpu.VMEM((2, tm * ROW_TILE, LANES), U32), pltpu.VMEM((tm, LANES), jnp.int32),
                        pltpu.SMEM((tm, LANES), jnp.int32), pltpu.VMEM((1, N_EXPERTS), F32),
                        pltpu.SemaphoreType.DMA((2,)), pltpu.SemaphoreType.DMA],
        input_output_aliases={len(args) - 1: 5} if aliased else {},
        compiler_params=_cparams(("arbitrary",)),
        name="outproj_route_dispatch",
    )(*args)


def _expert_plan(counts):
    experts = jnp.arange(N_EXPERTS, dtype=jnp.int32)
    nblk = (counts + MOE_BM - 1) // MOE_BM
    run_end = jnp.cumsum(nblk)
    run_start = run_end - nblk
    bi = jnp.arange(MOE_NB, dtype=jnp.int32)
    e_of = jnp.minimum(jnp.sum((run_end[None, :] <= bi[:, None]).astype(jnp.int32), axis=1), N_EXPERTS - 1)
    onehot = (e_of[:, None] == experts[None, :]).astype(jnp.int32)
    pick = lambda v: jnp.sum(onehot * v[None, :], axis=1)
    b_in = bi - pick(run_start)
    nonempty = nblk > 0
    later = nonempty[None, :] & (experts[None, :] > experts[:, None])
    nxt = jnp.min(jnp.where(later, experts[None, :], N_EXPERTS), axis=1)
    nxt = jnp.where(nxt == N_EXPERTS, -1, nxt)
    run_rank = jnp.cumsum(nonempty.astype(jnp.int32)) - 1
    i32 = lambda v: v.astype(jnp.int32)
    return (i32(e_of), i32(e_of * MOE_CAP_BLKS + b_in), i32(jnp.clip(pick(counts) - b_in * MOE_BM, 0, MOE_BM)),
            i32(b_in == 0), i32(pick(run_rank) & 1), i32(pick(nxt)), i32(run_end[-1:]))


def _expert_kernel(be_ref, bo_ref, nv_ref, bf_ref, bs_ref, bn_ref, nu_ref, x_ref, wg_hbm, wu_hbm, wd_hbm, o_ref,
                   wg_f, wu_f, wd_f, wg_s, wu_s, wd_s, sem, *, layer):
    i = pl.program_id(0)

    def weight_copies(e, slot):
        return (pltpu.make_async_copy(wg_hbm.at[layer, e], wg_f.at[slot], sem.at[slot]),
                pltpu.make_async_copy(wu_hbm.at[layer, e], wu_f.at[slot], sem.at[slot]),
                pltpu.make_async_copy(wd_hbm.at[layer, e], wd_f.at[slot], sem.at[slot]))

    @pl.when(i == 0)
    def _():
        for cp in weight_copies(be_ref[0], 0):
            cp.start(priority=1)

    @pl.when(i < nu_ref[0])
    def _():
        @pl.when(bf_ref[i] == 1)
        def _():
            slot = bs_ref[i]
            for cp in weight_copies(be_ref[i], slot):
                cp.wait()
            nxt = bn_ref[i]

            @pl.when(nxt >= 0)
            def _():
                for cp in weight_copies(nxt, 1 - slot):
                    cp.start(priority=1)

            wg_s[...] = wg_f[slot].astype(BF16)
            wu_s[...] = wu_f[slot].astype(BF16)
            wd_s[...] = wd_f[slot].astype(BF16)

        rows = lax.broadcasted_iota(jnp.int32, (MOE_BM, 1), 0)
        a, b = _unpack_rows(jnp.where(rows < nv_ref[i], _load_row_tiles(x_ref, MOE_BM), U32(0)))
        x = jnp.concatenate([a.astype(BF16), b.astype(BF16)], axis=1)
        g = _dot(x, wg_s[...])
        u = _dot(x, wu_s[...])
        _store_row_tiles(o_ref, _pack_rows(_dot((_silu(g) * u).astype(BF16), wd_s[...])))


def _expert_call(plan, xs, w_gate, w_up, w_down, layer):
    def rows_map(i, be, bo, nv, bf, bs, bn, nu):
        return (bo[jnp.minimum(i, nu[0] - 1)], 0)

    rows = pl.BlockSpec((MOE_BM * ROW_TILE, LANES), rows_map)
    hbm = pl.BlockSpec(memory_space=pl.ANY)
    grid_spec = pltpu.PrefetchScalarGridSpec(
        num_scalar_prefetch=7,
        grid=(MOE_NB,),
        in_specs=[rows, hbm, hbm, hbm],
        out_specs=rows,
        scratch_shapes=[pltpu.VMEM((2, D_MODEL, D_EXPERT), F32), pltpu.VMEM((2, D_MODEL, D_EXPERT), F32),
                        pltpu.VMEM((2, D_EXPERT, D_MODEL), F32),
                        pltpu.VMEM((D_MODEL, D_EXPERT), BF16), pltpu.VMEM((D_MODEL, D_EXPERT), BF16),
                        pltpu.VMEM((D_EXPERT, D_MODEL), BF16), pltpu.SemaphoreType.DMA((2,))],
    )
    return pl.pallas_call(
        functools.partial(_expert_kernel, layer=layer),
        out_shape=jax.ShapeDtypeStruct((MOE_SLOTS * ROW_TILE, LANES), U32),
        grid_spec=grid_spec,
        compiler_params=_cparams(("arbitrary",)),
        name="moe_experts",
    )(*plan, xs, w_gate, w_up, w_down)


def _combine_kernel(dest_ref, rows_ref, gate_ref, sh_ref, x1_ref, g2_ref, lg_ref, lb_ref, o_ref, gbuf, sem):
    tm = x1_ref.shape[0]
    i = pl.program_id(0)
    slot = lax.rem(i, 2)

    def issue(step, s):
        base = step * tm

        def body(t, carry):
            for k in range(TOP_K):
                d = dest_ref[(base + t) * TOP_K + k]
                pltpu.make_async_copy(rows_ref.at[_tile_rows(d)], gbuf.at[s, k, _tile_rows(t)],
                                      sem.at[s]).start(priority=k % 2)
            return carry

        lax.fori_loop(0, tm, body, 0)

    @pl.when(i == 0)
    def _():
        issue(0, 0)

    @pl.when(i + 1 < pl.num_programs(0))
    def _():
        issue(i + 1, 1 - slot)

    for k in range(TOP_K):
        pltpu.make_async_copy(rows_ref.at[pl.ds(0, tm * ROW_TILE)], gbuf.at[slot, k], sem.at[slot]).wait()
    gate = gate_ref[...]
    sh = sh_ref[...]
    lo = sh[:, :HALF]
    hi = sh[:, HALF:]
    for k in range(TOP_K):
        a, b = _unpack_rows(_load_row_tiles(gbuf.at[slot, k], tm))
        gk = gate[:, k:k + 1]
        lo = lo + gk * a
        hi = hi + gk * b
    moe = jnp.concatenate([lo, hi], axis=1)
    o_ref[...] = _layer_norm(ALPHA * x1_ref[...] + g2_ref[...] * moe, lg_ref[...], lb_ref[...])


def _combine_call(dest, out_rows, gate, sh, x1, ada, prm, rows_per_b, tm):
    m = x1.shape[0]
    r = ada.shape[1]
    row = pl.BlockSpec((tm, D_MODEL), lambda i, d: (i, 0))
    vec = pl.BlockSpec((1, D_MODEL), lambda i, d: (0, 0))
    grid_spec = pltpu.PrefetchScalarGridSpec(
        num_scalar_prefetch=1,
        grid=(m // tm,),
        in_specs=[pl.BlockSpec(memory_space=pl.ANY),
                  pl.BlockSpec((tm, LANES), lambda i, d: (i, 0)), row, row,
                  pl.BlockSpec((None, r, D_MODEL), _mod_spec(rows_per_b, tm, 5)), vec, vec],
        out_specs=row,
        scratch_shapes=[pltpu.VMEM((2, TOP_K, tm * ROW_TILE, LANES), U32), pltpu.SemaphoreType.DMA((2,))],
    )
    return pl.pallas_call(
        _combine_kernel,
        out_shape=jax.ShapeDtypeStruct((m, D_MODEL), F32),
        grid_spec=grid_spec,
        compiler_params=_cparams(("arbitrary",)),
        name="moe_combine",
    )(dest, out_rows, gate, sh, x1, ada, prm["ln2_g"], prm["ln2_b"])


def _block_diag(w):
    eye = jnp.eye(LRU_BLOCKS, dtype=w.dtype)
    full = jnp.einsum("hcd,hg->hcgd", w, eye).reshape(W_GROUP, W_GROUP)
    return full.astype(BF16)


def _rope_tables(pos):
    half = HEAD_DIM // 2
    freqs = ROPE_THETA ** (-jnp.arange(half, dtype=F32) / half)
    ang = pos.astype(F32)[:, None] * freqs[None, :]
    cos = jnp.cos(ang)
    sin = jnp.sin(ang)
    cos_t = jnp.concatenate([cos, cos, cos, cos], axis=1)
    sin_t = jnp.concatenate([-sin, sin, -sin, sin], axis=1)
    return cos_t, sin_t


def kernel(x_prompt, x_sample, cache_k, cache_v, state_conv_a, state_lru, state_conv_c, state_conv_d, page_table, c_prompt, c_sample, w_ada, b_ada, w_in, conv_a_w, conv_a_b, lru_wr, lru_br, lru_wi, lru_bi, lru_lambda, lam_q1, lam_k1, lam_q2, lam_k2, subln_g, conv_c_w, conv_c_b, ln_c_g, ln_c_b, conv_d_w, w_out, ln1_g, ln1_b, ln2_g, ln2_b, router_w, router_bias, exp_w_gate, exp_w_up, exp_w_down, sh_w_gate, sh_w_up, sh_w_down):
    n_pool = cache_k.shape[1]
    c_all = jnp.concatenate([c_prompt, c_sample, jnp.zeros((16 - BATCH - DEC_BATCH, D_MODEL), F32)], axis=0)
    ada_all = _ada_call(c_all, w_ada, b_ada)
    w_in_bf = w_in.astype(BF16)
    w_out_bf = w_out.astype(BF16)
    sh_wg_bf = sh_w_gate.astype(BF16)
    sh_wu_bf = sh_w_up.astype(BF16)
    sh_wd_bf = sh_w_down.astype(BF16)
    sh_w = (sh_wg_bf, sh_wu_bf, sh_wd_bf)
    cnt0 = jnp.zeros((1, N_EXPERTS), F32)
    cache_k4 = cache_k.reshape(DEPTH, n_pool, PAGE_SIZE * N_HEADS, LANES)
    cache_v4 = cache_v.reshape(DEPTH, n_pool, PAGE_SIZE * N_HEADS, LANES)
    cos_p, sin_p = _rope_tables(jnp.arange(SEQ, dtype=jnp.int32))
    cos_s, sin_s = _rope_tables(jnp.full((1,), PAST_LEN, jnp.int32))
    lane = jnp.arange(LANES)
    comp_mask = jnp.stack([lane < HEAD_DIM, lane >= HEAD_DIM]).astype(F32)

    xp = x_prompt.reshape(T_PROMPT, D_MODEL)
    xs = x_sample.reshape(DEC_BATCH, D_MODEL)
    outs = [[] for _ in range(12)]
    for l in range(DEPTH):
        lam_init = 0.8 - 0.6 * math.exp(-0.3 * l)
        prm = dict(
            conv_a_w=conv_a_w[l], conv_a_b=conv_a_b[l][None], wr=_block_diag(lru_wr[l]), br=lru_br[l][None],
            wi=_block_diag(lru_wi[l]), bi=lru_bi[l][None], lam=lru_lambda[l][None],
            conv_c_w=conv_c_w[l], conv_c_b=conv_c_b[l][None], ln_c_g=ln_c_g[l][None], ln_c_b=ln_c_b[l][None],
            conv_d_w=conv_d_w[l],
            lam_q1=lam_q1[l][None], lam_k1=lam_k1[l][None], lam_q2=lam_q2[l][None], lam_k2=lam_k2[l][None],
            subln_g=subln_g[l][None], subln_g_col=subln_g[l][:, None], lam_init=jnp.full((1, 1), lam_init, F32),
            ln1_g=ln1_g[l][None], ln1_b=ln1_b[l][None], ln2_g=ln2_g[l][None], ln2_b=ln2_b[l][None],
            router_w_bf=router_w[l].astype(BF16), router_bias=router_bias[l][None],
        )
        ada_p = ada_all[l, :BATCH].reshape(BATCH, 1, 6 * D_MODEL)
        ada_s = ada_all[l, BATCH:BATCH + DEC_BATCH].reshape(1, DEC_BATCH, 6 * D_MODEL)

        proj_p = _inproj_call(xp, ada_p, w_in_bf, l, SEQ, 1024, 1024).reshape(BATCH, SEQ, D_IN)
        ya, yc, yd, q_p, k_p, kh_p, vh_p, sa, sh, sc, sd = _mixer_prompt_call(proj_p, cos_p, sin_p, prm)
        yb = _flash_call(q_p, k_p, proj_p, prm)
        outs[0].append(kh_p)
        outs[1].append(vh_p)
        outs[2].append(sa)
        outs[3].append(sh.reshape(BATCH, W_GROUP))
        outs[4].append(sc)
        outs[5].append(sd)

        proj_s = _inproj_call(xs, ada_s, w_in_bf, l, DEC_BATCH, DEC_BATCH, 512)
        ya_s, yc_s, yd_s, q_s, k_s, sa_s, sh_s, sc_s, sd_s = _mixer_sample_call(
            proj_s, cos_s, sin_s, prm,
            state_conv_a[l].transpose(1, 0, 2), state_lru[l], state_conv_c[l].transpose(1, 0, 2),
            state_conv_d[l].transpose(1, 0, 2))
        v_s = proj_s[:, 4 * W_GROUP:5 * W_GROUP]
        q4 = q_s.reshape(DEC_BATCH, 1, N_HEADS, LANES) * (HEAD_DIM ** -0.5)
        q8 = (q4 * comp_mask[None, :, None, :]).reshape(DEC_BATCH, 2 * N_HEADS, LANES).astype(BF16)
        pad4 = jnp.zeros((DEC_BATCH, N_HEADS, LANES), F32)
        k_new8 = jnp.concatenate([k_s.reshape(DEC_BATCH, N_HEADS, LANES), pad4], axis=1)
        v_new8 = jnp.concatenate([v_s.reshape(DEC_BATCH, N_HEADS, LANES), pad4], axis=1)
        yb_s = _decode_call(q8, k_new8, v_new8, cache_k4, cache_v4, page_table, prm, l)
        yb_s = yb_s.reshape(DEC_BATCH, W_GROUP).astype(BF16)
        outs[6].append(k_s.reshape(DEC_BATCH, 1, N_HEADS, 2 * HEAD_DIM))
        outs[7].append(v_s.reshape(DEC_BATCH, 1, N_HEADS, 2 * HEAD_DIM))
        outs[8].append(sa_s.transpose(1, 0, 2))
        outs[9].append(sh_s)
        outs[10].append(sc_s.transpose(1, 0, 2))
        outs[11].append(sd_s.transpose(1, 0, 2))

        flat = lambda t: t.reshape(T_PROMPT, W_GROUP)
        x1_p, shared_p, gate_p, dest_p, cnt_p, x_sorted = _outproj_call(
            flat(ya), flat(yb), flat(yc), flat(yd), w_out_bf, xp, ada_p, prm, sh_w, cnt0, l, SEQ, 256)
        x1_s, shared_s, gate_s, dest_s, cnt_all, x_sorted = _outproj_call(
            ya_s, yb_s, yc_s, yd_s, w_out_bf, xs, ada_s, prm, sh_w, cnt_p, l, DEC_BATCH, DEC_BATCH, xs_prev=x_sorted)

        plan = _expert_plan(cnt_all[0].astype(jnp.int32))
        out_rows = _expert_call(plan, x_sorted, exp_w_gate, exp_w_up, exp_w_down, l)
        xp = _combine_call(dest_p[:, :TOP_K].reshape(-1), out_rows, gate_p, shared_p, x1_p, ada_p, prm, SEQ, 256)
        xs = _combine_call(dest_s[:, :TOP_K].reshape(-1), out_rows, gate_s, shared_s, x1_s, ada_s, prm, DEC_BATCH, DEC_BATCH)

    st = [jnp.stack(o, axis=0) for o in outs]
    kv_shape = (DEPTH, BATCH, SEQ // PAGE_SIZE, PAGE_SIZE, N_HEADS, 2 * HEAD_DIM)
    return (xp.reshape(BATCH, SEQ, D_MODEL), xs.reshape(DEC_BATCH, 1, D_MODEL),
            st[0].reshape(kv_shape), st[1].reshape(kv_shape), st[2], st[3], st[4], st[5],
            st[6], st[7], st[8], st[9], st[10], st[11])
```
